```python
import math
import functools
import jax
import jax.numpy as jnp
from jax import lax
import numpy as np

D_MODEL = 1024
BATCH = 32
SEQ = 2048
DEPTH = 1
DEC_BATCH = 128
DEC_SEQ = 4
PAST_LEN = 8192
PAGE_SIZE = 128

HG_HEADS = 8
HG_KEY = 128
HG_VAL = D_MODEL // HG_HEADS
HG_CHUNK = 32
DA_HEADS = 8
DA_HEAD_DIM = D_MODEL // (2 * DA_HEADS)
DA_VAL = 2 * DA_HEAD_DIM
Q_BLOCK = 128
REL_BUCKETS = 32
REL_MAX_DIST = 128
FFN_DIM = ((8 * D_MODEL // 3 + 255) // 256) * 256
LN_EPS = 1e-5
RMS_EPS = 1e-5
DEEPNORM_ALPHA = (2.0 * DEPTH) ** 0.25
DEEPNORM_BETA = (8.0 * DEPTH) ** -0.25
HG_QK_W = HG_HEADS * HG_KEY
HG_V_W = HG_HEADS * HG_VAL
DA_QK_W = DA_HEADS * 2 * DA_HEAD_DIM
DA_V_W = DA_HEADS * DA_VAL
IN_SPLITS = (HG_QK_W, HG_QK_W, HG_V_W, HG_V_W, DA_QK_W, DA_QK_W, DA_V_W, D_MODEL, D_MODEL)
IN_WIDTH = sum(IN_SPLITS)
N_PAGES = PAST_LEN // PAGE_SIZE
N_PAGES_USED = DEC_BATCH * N_PAGES
N_POOL_PAGES = N_PAGES_USED + max(1, N_PAGES_USED // 4)

kernel_name = 'hybrid_hgrn2_diffattn_decoder_step'


def layer_norm(x, g, b):
    xf = x.astype(jnp.float32)
    mu = jnp.mean(xf, axis=-1, keepdims=True)
    var = jnp.mean(jnp.square(xf - mu), axis=-1, keepdims=True)
    return ((xf - mu) * lax.rsqrt(var + LN_EPS) * g + b).astype(x.dtype)


def rms_norm(x, g):
    xf = x.astype(jnp.float32)
    return (xf * lax.rsqrt(jnp.mean(jnp.square(xf), axis=-1, keepdims=True) + RMS_EPS) * g).astype(x.dtype)


def swiglu(h, w_in, w_out):
    gate, up = jnp.split(h @ w_in, 2, axis=-1)
    return (jax.nn.silu(gate) * up) @ w_out


def rel_bias(q_pos, k_pos, table):
    n = jnp.maximum(q_pos[:, None] - k_pos[None, :], 0)
    max_exact = REL_BUCKETS // 2
    nf = jnp.maximum(n, 1).astype(jnp.float32)
    large = max_exact + (jnp.log(nf / max_exact) / math.log(REL_MAX_DIST / max_exact)
                         * (REL_BUCKETS - max_exact)).astype(jnp.int32)
    bucket = jnp.where(n < max_exact, n, jnp.minimum(large, REL_BUCKETS - 1))
    return table.astype(jnp.float32)[bucket].transpose(3, 2, 0, 1)


def diff_attn_core(q, q_pos, segs, lam, table):
    qs = q.astype(jnp.float32) * DA_HEAD_DIM ** -0.5
    logits = []
    for k, _, k_pos in segs:
        s = jnp.einsum('bqhmd,bkhmd->bhmqk', qs, k.astype(jnp.float32)) + rel_bias(q_pos, k_pos, table)
        logits.append(jnp.where(k_pos[None, :] <= q_pos[:, None], s, -jnp.inf))
    p = jax.nn.softmax(jnp.concatenate(logits, axis=-1), axis=-1)
    attn = p[:, :, 0] - lam * p[:, :, 1]
    out = None
    start = 0
    for k, v, k_pos in segs:
        n = k_pos.shape[0]
        part = jnp.einsum('bhqk,bkhe->bqhe', attn[..., start:start + n], v.astype(jnp.float32))
        out = part if out is None else out + part
        start += n
    return out


def attend_prompt(q, k, v, lam, table):
    B, T = q.shape[0], q.shape[1]
    nb = T // Q_BLOCK
    pos = jnp.arange(T, dtype=jnp.int32)
    qb = q.reshape(B, nb, Q_BLOCK, DA_HEADS, 2, DA_HEAD_DIM).swapaxes(0, 1)
    qpb = pos.reshape(nb, Q_BLOCK)

    def one_block(args):
        qi, qp = args
        return diff_attn_core(qi, qp, [(k, v, pos)], lam, table)

    o = lax.map(one_block, (qb, qpb))
    return o.swapaxes(0, 1).reshape(B, T, DA_HEADS, DA_VAL)


def attend_sample(q, k, v, lam, table, k_cache, v_cache, page_table):
    Bd, T = q.shape[0], q.shape[1]
    past = page_table.shape[1] * k_cache.shape[1]
    k_past = k_cache[page_table].reshape(Bd, past, DA_HEADS, 2, DA_HEAD_DIM)
    v_past = v_cache[page_table].reshape(Bd, past, DA_HEADS, DA_VAL)
    past_pos = jnp.arange(past, dtype=jnp.int32)
    new_pos = past + jnp.arange(T, dtype=jnp.int32)
    return diff_attn_core(q, new_pos, [(k_past, v_past, past_pos), (k, v, new_pos)], lam, table)


def hgrn2_chunked(q, k, v, logf, s0):
    B, T, H, dk = q.shape
    dv = v.shape[-1]
    C = math.gcd(T, HG_CHUNK)
    n = T // C
    to_chunks = lambda a: a.reshape(B, n, C, H, a.shape[-1]).transpose(1, 0, 3, 2, 4)
    mask = jnp.tril(jnp.ones((C, C), dtype=bool))[:, :, None]

    def step(S, inp):
        qc, kc, vc, gc = inp
        b = jnp.cumsum(gc, axis=2)
        inter = jnp.einsum('bhtk,bhkv->bhtv', qc * jnp.exp(b), S)
        rel = b[:, :, :, None, :] - b[:, :, None, :, :]
        decay = jnp.where(mask, jnp.exp(jnp.minimum(rel, 0.0)), 0.0)
        A = jnp.einsum('bhtk,bhtsk,bhsk->bhts', qc, decay, kc)
        intra = jnp.einsum('bhts,bhsv->bhtv', A, vc)
        b_last = b[:, :, -1]
        S_new = jnp.exp(b_last)[..., None] * S + jnp.einsum(
            'bhsk,bhsv->bhkv', kc * jnp.exp(b_last[:, :, None, :] - b), vc)
        return S_new, inter + intra

    S, o = lax.scan(step, s0.astype(jnp.float32), (to_chunks(q), to_chunks(k), to_chunks(v), to_chunks(logf)))
    return o.transpose(1, 0, 3, 2, 4).reshape(B, T, H, dv), S


def decoder_layer(x, c, s0, attend, lam, lam_init, lb, lw):
    B, T, _ = x.shape
    f32 = jnp.float32
    ada = (jax.nn.silu(c.astype(f32)) @ lw['w_ada'].astype(f32) + lw['b_ada']).reshape(B, 3, 3, 1, D_MODEL)
    ada = ada.astype(x.dtype)
    modulate = lambda h, j: h * (1 + ada[:, j, 1]) + ada[:, j, 0]
    f1 = swiglu(modulate(x, 0), lw['ffn_w_in'][0], lw['ffn_w_out'][0])
    x = layer_norm(DEEPNORM_ALPHA * x + 0.5 * ada[:, 0, 2] * f1, lw['ln_g'][0], lw['ln_b'][0])
    z = modulate(x, 1) @ lw['w_in']
    offs = np.cumsum(IN_SPLITS)[:-1].tolist()
    hq, hf, hi, hg, dq, dk, dv, ga, gb = jnp.split(z, offs, axis=-1)
    hshape = (B, T, HG_HEADS, HG_KEY)
    fgate = lb + (1 - lb) * jax.nn.sigmoid(hf.astype(f32))
    q_a = (jax.nn.silu(hq.astype(f32)) * HG_KEY ** -0.5).reshape(hshape)
    k_a = (1 - fgate).reshape(hshape)
    logf = jnp.log(fgate).reshape(hshape)
    v_a = hi.astype(f32).reshape(B, T, HG_HEADS, HG_VAL)
    o_a, s_new = hgrn2_chunked(q_a, k_a, v_a, logf, s0)
    o_a = rms_norm(o_a, lw['hg_norm_g']) * jax.nn.silu(hg.astype(f32)).reshape(B, T, HG_HEADS, HG_VAL)
    o_a = o_a.reshape(B, T, HG_V_W).astype(x.dtype)
    q_b = dq.reshape(B, T, DA_HEADS, 2, DA_HEAD_DIM)
    k_b = dk.reshape(B, T, DA_HEADS, 2, DA_HEAD_DIM)
    v_b = dv.reshape(B, T, DA_HEADS, DA_VAL)
    o_b = attend(q_b, k_b, v_b, lam)
    o_b = (rms_norm(o_b, lw['da_subln_g']) * (1 - lam_init)).reshape(B, T, DA_V_W).astype(x.dtype)
    m = jax.nn.sigmoid(ga) * (o_a @ lw['w_pa']) + jax.nn.sigmoid(gb) * (o_b @ lw['w_pb'])
    x = layer_norm(DEEPNORM_ALPHA * x + ada[:, 1, 2] * (m @ lw['w_o']), lw['ln_g'][1], lw['ln_b'][1])
    f2 = swiglu(modulate(x, 2), lw['ffn_w_in'][1], lw['ffn_w_out'][1])
    x = layer_norm(DEEPNORM_ALPHA * x + 0.5 * ada[:, 2, 2] * f2, lw['ln_g'][2], lw['ln_b'][2])
    return x, k_b.reshape(B, T, DA_HEADS, 2 * DA_HEAD_DIM), v_b, s_new


def setup_inputs(seed: int = 0) -> dict:
    key = jax.random.key(seed)
    ks = jax.random.split(key, 28)
    f32 = jnp.float32
    nrm = lambda k, shape, s=1.0: jax.random.normal(k, shape, f32) * s
    col_scale = np.concatenate([np.full(w, DEEPNORM_BETA if i in (2, 6) else 1.0, np.float32)
                                for i, w in enumerate(IN_SPLITS)])
    perm = jax.random.permutation(ks[5], N_POOL_PAGES)[:N_PAGES_USED]
    return {
        'x_prompt': nrm(ks[0], (BATCH, SEQ, D_MODEL)),
        'x_sample': nrm(ks[1], (DEC_BATCH, DEC_SEQ, D_MODEL)),
        'cache_k': nrm(ks[2], (DEPTH, N_POOL_PAGES, PAGE_SIZE, DA_HEADS, 2 * DA_HEAD_DIM)),
        'cache_v': nrm(ks[3], (DEPTH, N_POOL_PAGES, PAGE_SIZE, DA_HEADS, DA_VAL), DEEPNORM_BETA),
        'state_hgrn': nrm(ks[4], (DEPTH, DEC_BATCH, HG_HEADS, HG_KEY, HG_VAL), 0.3 * DEEPNORM_BETA),
        'page_table': perm.reshape(DEC_BATCH, N_PAGES).astype(jnp.int32),
        'c_prompt': nrm(ks[6], (BATCH, D_MODEL)),
        'c_sample': nrm(ks[7], (DEC_BATCH, D_MODEL)),
        'w_ada': nrm(ks[8], (DEPTH, D_MODEL, 9 * D_MODEL), D_MODEL ** -0.5),
        'b_ada': nrm(ks[9], (DEPTH, 9 * D_MODEL), 0.01),
        'ln_g': 1.0 + nrm(ks[10], (DEPTH, 3, D_MODEL), 0.01),
        'ln_b': nrm(ks[11], (DEPTH, 3, D_MODEL), 0.01),
        'ffn_w_in': nrm(ks[12], (DEPTH, 2, D_MODEL, 2 * FFN_DIM), D_MODEL ** -0.5),
        'ffn_w_out': nrm(ks[13], (DEPTH, 2, FFN_DIM, D_MODEL), DEEPNORM_BETA * FFN_DIM ** -0.5),
        'w_in': nrm(ks[14], (DEPTH, D_MODEL, IN_WIDTH), D_MODEL ** -0.5) * jnp.asarray(col_scale),
        'hg_lower_bound': nrm(ks[15], (DEPTH + 1, HG_QK_W)),
        'hg_norm_g': 1.0 + nrm(ks[16], (DEPTH, HG_VAL), 0.01),
        'da_lambda_q1': nrm(ks[17], (DEPTH, DA_HEAD_DIM), 0.1),
        'da_lambda_k1': nrm(ks[18], (DEPTH, DA_HEAD_DIM), 0.1),
        'da_lambda_q2': nrm(ks[19], (DEPTH, DA_HEAD_DIM), 0.1),
        'da_lambda_k2': nrm(ks[20], (DEPTH, DA_HEAD_DIM), 0.1),
        'da_subln_g': 1.0 + nrm(ks[21], (DEPTH, DA_VAL), 0.01),
        'rel_bias': nrm(ks[22], (REL_BUCKETS, 2, DA_HEADS), 0.5),
        'w_pa': nrm(ks[23], (DEPTH, HG_V_W, D_MODEL), DEEPNORM_BETA * HG_V_W ** -0.5),
        'w_pb': nrm(ks[24], (DEPTH, DA_V_W, D_MODEL), DEEPNORM_BETA * DA_V_W ** -0.5),
        'w_o': nrm(ks[25], (DEPTH, D_MODEL, D_MODEL), DEEPNORM_BETA * D_MODEL ** -0.5),
    }


def reference(x_prompt, x_sample, cache_k, cache_v, state_hgrn, page_table, c_prompt, c_sample,
              w_ada, b_ada, ln_g, ln_b, ffn_w_in, ffn_w_out, w_in, hg_lower_bound, hg_norm_g,
              da_lambda_q1, da_lambda_k1, da_lambda_q2, da_lambda_k2, da_subln_g, rel_bias,
              w_pa, w_pb, w_o):
    f32 = jnp.float32
    lower_bounds = jnp.cumsum(jax.nn.softmax(hg_lower_bound.astype(f32), axis=0), axis=0)
    y_p, y_s = x_prompt, x_sample
    kp_list, vp_list, sp_list, ks_list, vs_list, ss_list = [], [], [], [], [], []
    for l in range(DEPTH):
        lw = {'w_ada': w_ada[l], 'b_ada': b_ada[l], 'ln_g': ln_g[l], 'ln_b': ln_b[l],
              'ffn_w_in': ffn_w_in[l], 'ffn_w_out': ffn_w_out[l], 'w_in': w_in[l],
              'hg_norm_g': hg_norm_g[l], 'da_subln_g': da_subln_g[l],
              'w_pa': w_pa[l], 'w_pb': w_pb[l], 'w_o': w_o[l]}
        lam_init = 0.8 - 0.6 * math.exp(-0.3 * l)
        lam = (jnp.exp(jnp.sum(da_lambda_q1[l].astype(f32) * da_lambda_k1[l].astype(f32)))
               - jnp.exp(jnp.sum(da_lambda_q2[l].astype(f32) * da_lambda_k2[l].astype(f32))) + lam_init)
        s0_p = jnp.zeros((x_prompt.shape[0], HG_HEADS, HG_KEY, HG_VAL), f32)
        attend_p = functools.partial(attend_prompt, table=rel_bias)
        y_p, k_p, v_p, s_p = decoder_layer(y_p, c_prompt, s0_p, attend_p, lam, lam_init, lower_bounds[l], lw)
        attend_s = functools.partial(attend_sample, table=rel_bias, k_cache=cache_k[l],
                                     v_cache=cache_v[l], page_table=page_table)
        y_s, k_s, v_s, s_s = decoder_layer(y_s, c_sample, state_hgrn[l], attend_s, lam, lam_init, lower_bounds[l], lw)
        kp_list.append(k_p); vp_list.append(v_p); sp_list.append(s_p)
        ks_list.append(k_s); vs_list.append(v_s); ss_list.append(s_s)
    k_prompt = jnp.stack(kp_list)
    v_prompt = jnp.stack(vp_list)
    hgrn_state_prompt = jnp.stack(sp_list)
    k_sample = jnp.stack(ks_list)
    v_sample = jnp.stack(vs_list)
    hgrn_state_sample = jnp.stack(ss_list)
    return (y_p, y_s, k_prompt, v_prompt, hgrn_state_prompt, k_sample, v_sample, hgrn_state_sample)
```

```python
import functools
import math

import jax
import jax.numpy as jnp
from jax import lax
from jax.experimental import pallas as pl
from jax.experimental.pallas import tpu as pltpu

F32 = jnp.float32
BF16 = jnp.bfloat16

LN_EPS = 1e-5
RMS_EPS = 1e-5
REL_BUCKETS = 32
REL_MAX_DIST = 128
MASKED = -1e30
VMEM_LIMIT_BYTES = 56 * 1024 * 1024
LANES = 128


def _cparams(sem):
    return pltpu.CompilerParams(dimension_semantics=sem, vmem_limit_bytes=VMEM_LIMIT_BYTES)


def _sigmoid(x):
    return 1.0 / (1.0 + jnp.exp(-x))


def _dot(a, b):
    return jnp.dot(a, b, preferred_element_type=F32)


def _dot_nt(a, b):
    return lax.dot_general(a, b, (((1,), (1,)), ((), ())), preferred_element_type=F32)


def _dot_tn(a, b):
    return lax.dot_general(a, b, (((0,), (0,)), ((), ())), preferred_element_type=F32)


def _layer_norm(x, g, b):
    mu = jnp.mean(x, axis=-1, keepdims=True)
    xc = x - mu
    var = jnp.mean(xc * xc, axis=-1, keepdims=True)
    return xc * lax.rsqrt(var + LN_EPS) * g + b


def _ada_kernel(c_ref, w_ref, b_ref, o_ref):
    c = c_ref[...]
    s = c * _sigmoid(c)
    o_ref[...] = jnp.dot(s, w_ref[...], preferred_element_type=F32,
                         precision=lax.Precision.HIGHEST) + b_ref[...]


def _ada_call(c, w, b):
    rows, d = c.shape
    width = w.shape[1]
    tn = 1152 if width % 1152 == 0 else width
    return pl.pallas_call(
        _ada_kernel,
        grid=(width // tn,),
        in_specs=[pl.BlockSpec((rows, d), lambda j: (0, 0)),
                  pl.BlockSpec((d, tn), lambda j: (0, j)),
                  pl.BlockSpec((1, tn), lambda j: (0, j))],
        out_specs=pl.BlockSpec((rows, tn), lambda j: (0, j)),
        out_shape=jax.ShapeDtypeStruct((rows, width), F32),
        compiler_params=_cparams(("arbitrary",)),
        name="ada_proj",
    )(c, w, b.reshape(1, width))


def _ada_spec(ada4, k, tm, tiles_per_seq):
    d = ada4.shape[-1]
    if ada4.shape[2] == 1:
        return pl.BlockSpec((None, None, 1, d), lambda i, *_: (k, i // tiles_per_seq, 0, 0))
    return pl.BlockSpec((None, None, tm, d), lambda i, *_: (k, 0, i, 0))


def _ffn_ln_kernel(x_ref, sh_ref, sc_ref, gt_ref, wg_ref, wu_ref, wo_ref, lg_ref, lb_ref, o_ref,
                   h_scr, acc_scr, *, alpha, nk):
    k = pl.program_id(1)

    @pl.when(k == 0)
    def _():
        h_scr[...] = (x_ref[...] * (1.0 + sc_ref[...]) + sh_ref[...]).astype(BF16)

    h = h_scr[...]
    gate = _dot(h, wg_ref[...])
    up = _dot(h, wu_ref[...])
    a = (gate * _sigmoid(gate) * up).astype(BF16)
    part = _dot(a, wo_ref[...])

    @pl.when(k == 0)
    def _():
        acc_scr[...] = part

    @pl.when(k > 0)
    def _():
        acc_scr[...] += part

    @pl.when(k == nk - 1)
    def _():
        y = alpha * x_ref[...] + 0.5 * gt_ref[...] * acc_scr[...]
        o_ref[...] = _layer_norm(y, lg_ref[...], lb_ref[...])


def _ffn_ln_call(x, ada4, sub, w_in, w_out, ln_g, ln_b, *, alpha, tm, tiles_per_seq):
    n, d = x.shape
    f = w_out.shape[0]
    tf = 1408 if f % 1408 == 0 else f
    nk = f // tf
    kern = functools.partial(_ffn_ln_kernel, alpha=alpha, nk=nk)
    return pl.pallas_call(
        kern,
        grid=(n // tm, nk),
        in_specs=[pl.BlockSpec((tm, d), lambda i, k: (i, 0)),
                  _ada_spec(ada4, 3 * sub + 0, tm, tiles_per_seq),
                  _ada_spec(ada4, 3 * sub + 1, tm, tiles_per_seq),
                  _ada_spec(ada4, 3 * sub + 2, tm, tiles_per_seq),
                  pl.BlockSpec((d, tf), lambda i, k: (0, k)),
                  pl.BlockSpec((d, tf), lambda i, k: (0, nk + k)),
                  pl.BlockSpec((tf, d), lambda i, k: (k, 0)),
                  pl.BlockSpec((1, d), lambda i, k: (0, 0)),
                  pl.BlockSpec((1, d), lambda i, k: (0, 0))],
        out_specs=pl.BlockSpec((tm, d), lambda i, k: (i, 0)),
        out_shape=jax.ShapeDtypeStruct((n, d), F32),
        scratch_shapes=[pltpu.VMEM((tm, d), BF16), pltpu.VMEM((tm, d), F32)],
        compiler_params=_cparams(("parallel", "arbitrary")),
        name="ffn_ln",
    )(x, ada4, ada4, ada4, w_in, w_in, w_out, ln_g.reshape(1, d), ln_b.reshape(1, d))


def _inproj_kernel(x_ref, sh_ref, sc_ref, w_ref, lbp_ref,
                   qa_ref, lf_ref, ka_ref, va_ref, ga_ref, q16_ref, kb_ref, k16_ref, vb_ref, v16_ref,
                   sga_ref, sgb_ref, h_scr, *, layer, hg_scale, da_scale):
    j = pl.program_id(1)

    @pl.when(j == 0)
    def _():
        h_scr[...] = (x_ref[...] * (1.0 + sc_ref[...]) + sh_ref[...]).astype(BF16)

    z = _dot(h_scr[...], w_ref[...])

    @pl.when(j == 0)
    def _():
        qa_ref[...] = (z * _sigmoid(z) * hg_scale).astype(BF16)

    @pl.when(j == 1)
    def _():
        p = lbp_ref[...]
        e = jnp.exp(p - jnp.max(p, axis=0, keepdims=True))
        lb = jnp.sum(e[:layer + 1], axis=0, keepdims=True) / jnp.sum(e, axis=0, keepdims=True)
        f = lb + (1.0 - lb) * _sigmoid(z)
        lf_ref[...] = jnp.log(f)
        ka_ref[...] = (1.0 - f).astype(BF16)

    @pl.when(j == 2)
    def _():
        va_ref[...] = z.astype(BF16)

    @pl.when(j == 3)
    def _():
        ga_ref[...] = (z * _sigmoid(z)).astype(BF16)

    @pl.when(j == 4)
    def _():
        q16_ref[...] = (z * da_scale).astype(BF16)

    @pl.when(j == 5)
    def _():
        kb_ref[...] = z
        k16_ref[...] = z.astype(BF16)

    @pl.when(j == 6)
    def _():
        vb_ref[...] = z
        v16_ref[...] = z.astype(BF16)

    @pl.when(j == 7)
    def _():
        sga_ref[...] = _sigmoid(z).astype(BF16)

    @pl.when(j == 8)
    def _():
        sgb_ref[...] = _sigmoid(z).astype(BF16)


def _inproj_call(x, ada4, w_in, lb_params, *, layer, hg_key, da_head_dim, tm, tiles_per_seq):
    n, d = x.shape
    nj = w_in.shape[1] // d
    assert nj == 9
    kern = functools.partial(_inproj_kernel, layer=layer, hg_scale=hg_key ** -0.5,
                             da_scale=da_head_dim ** -0.5)
    out_dtypes = [BF16, F32, BF16, BF16, BF16, BF16, F32, BF16, F32, BF16, BF16, BF16]
    row_spec = pl.BlockSpec((tm, d), lambda i, j: (i, 0))
    return pl.pallas_call(
        kern,
        grid=(n // tm, nj),
        in_specs=[row_spec,
                  _ada_spec(ada4, 3, tm, tiles_per_seq),
                  _ada_spec(ada4, 4, tm, tiles_per_seq),
                  pl.BlockSpec((d, d), lambda i, j: (0, j)),
                  pl.BlockSpec(lb_params.shape, lambda i, j: (0, 0))],
        out_specs=[row_spec] * len(out_dtypes),
        out_shape=[jax.ShapeDtypeStruct((n, d), dt) for dt in out_dtypes],
        scratch_shapes=[pltpu.VMEM((tm, d), BF16)],
        compiler_params=_cparams(("parallel", "arbitrary")),
        name="in_proj",
    )(x, ada4, ada4, w_in, lb_params)


def _hgrn_kernel(*refs, chunk, heads, has_s0):
    if has_s0:
        q_ref, g_ref, k_ref, v_ref, gt_ref, ng_ref, s0_ref, o_ref, s_ref = refs
    else:
        q_ref, g_ref, k_ref, v_ref, gt_ref, ng_ref, o_ref, s_ref = refs
        s0_ref = None
    C = chunk
    W = g_ref.shape[-1]

    @pl.when(pl.program_id(1) == 0)
    def _():
        if has_s0:
            s_ref[...] = s0_ref[...]
        else:
            s_ref[...] = jnp.zeros(s_ref.shape, F32)

    g = g_ref[0]
    row = lax.broadcasted_iota(jnp.int32, (C, W), 0)
    b = g
    sh = 1
    while sh < C:
        b = b + jnp.where(row >= sh, pltpu.roll(b, sh, 0), 0.0)
        sh *= 2

    ti = lax.broadcasted_iota(jnp.int32, (C, C), 0)
    si = lax.broadcasted_iota(jnp.int32, (C, C), 1)
    levels = []
    u = b
    hh = 1
    while hh < C:
        right = (row & (2 * hh - 1)) >= hh
        ref = jnp.where(right, pltpu.roll(u, hh, 0), u)
        e = jnp.exp(-jnp.abs(b - ref))
        pair = (((ti ^ si) & ~(2 * hh - 1)) == 0) & ((ti & (2 * hh - 1)) >= hh) & ((si & (2 * hh - 1)) < hh)
        levels.append((e, pair))
        u = jnp.where(right, u, pltpu.roll(u, C - hh, 0))
        hh *= 2
    b_last = u

    q = q_ref[0].astype(F32)
    k = k_ref[0].astype(F32)
    v = v_ref[0]
    gt = gt_ref[0].astype(F32)
    q_in = (q * jnp.exp(b)).astype(BF16)
    k_st = (k * jnp.exp(b_last - b)).astype(BF16)
    qk = q * k
    q_lv = [(q * e).astype(BF16) for e, _ in levels]
    k_lv = [(k * e).astype(BF16) for e, _ in levels]
    eye = lax.broadcasted_iota(jnp.int32, (LANES, LANES), 0) == lax.broadcasted_iota(jnp.int32, (LANES, LANES), 1)
    ng = ng_ref[...]

    for h in range(heads):
        sl = slice(h * LANES, (h + 1) * LANES)
        a = jnp.zeros((C, C), F32)
        for lv, (_, pair) in enumerate(levels):
            a = a + jnp.where(pair, _dot_nt(q_lv[lv][:, sl], k_lv[lv][:, sl]), 0.0)
        s_old = s_ref[0, h]
        vh = v[:, sl]
        diag = jnp.sum(qk[:, sl], axis=1, keepdims=True)
        o = _dot(q_in[:, sl], s_old.astype(BF16)) + _dot(a.astype(BF16), vh) + diag * vh.astype(F32)
        ms = jnp.mean(o * o, axis=1, keepdims=True)
        o_ref[0, :, sl] = (o * lax.rsqrt(ms + RMS_EPS) * ng * gt[:, sl]).astype(BF16)
        bl_col = jnp.sum(jnp.where(eye, jnp.broadcast_to(b_last[0:1, sl], (LANES, LANES)), 0.0),
                         axis=1, keepdims=True)
        s_ref[0, h] = jnp.exp(bl_col) * s_old + _dot_tn(k_st[:, sl], vh)


def _hgrn_call(qa, lf, ka, va, ga, norm_g, s0, *, chunk):
    bsz, t, w = qa.shape
    heads = w // LANES
    has_s0 = s0 is not None
    kern = functools.partial(_hgrn_kernel, chunk=chunk, heads=heads, has_s0=has_s0)
    tok = pl.BlockSpec((1, chunk, w), lambda b, c: (b, c, 0))
    st = pl.BlockSpec((1, heads, LANES, LANES), lambda b, c: (b, 0, 0, 0))
    in_specs = [tok, tok, tok, tok, tok, pl.BlockSpec((1, LANES), lambda b, c: (0, 0))]
    args = [qa, lf, ka, va, ga, norm_g.reshape(1, LANES)]
    if has_s0:
        in_specs.append(st)
        args.append(s0)
    return pl.pallas_call(
        kern,
        grid=(bsz, t // chunk),
        in_specs=in_specs,
        out_specs=[tok, st],
        out_shape=[jax.ShapeDtypeStruct((bsz, t, w), BF16),
                   jax.ShapeDtypeStruct((bsz, heads, LANES, LANES), F32)],
        compiler_params=_cparams(("parallel", "arbitrary")),
        name="hgrn2",
    )(*args)


def _relbias_kernel(tab_ref, o_ref, *, off):
    hm = pl.program_id(0)
    rows, cols = o_ref.shape[1], o_ref.shape[2]
    n = off + lax.broadcasted_iota(jnp.int32, (rows, cols), 0) - lax.broadcasted_iota(jnp.int32, (rows, cols), 1)
    nn = jnp.maximum(n, 0)
    max_exact = REL_BUCKETS // 2
    nf = jnp.maximum(nn, 1).astype(F32)
    large = max_exact + (jnp.log(nf / max_exact) / math.log(REL_MAX_DIST / max_exact)
                         * (REL_BUCKETS - max_exact)).astype(jnp.int32)
    bucket = jnp.where(nn < max_exact, nn, jnp.minimum(large, REL_BUCKETS - 1))
    val = jnp.zeros((rows, cols), F32)
    for bk in range(REL_BUCKETS):
        val = jnp.where(bucket == bk, tab_ref[hm * REL_BUCKETS + bk], val)
    o_ref[0] = jnp.where(n < 0, MASKED, val)


def _relbias_call(tab_flat, n_hm, rows, cols, off):
    return pl.pallas_call(
        functools.partial(_relbias_kernel, off=off),
        grid=(n_hm,),
        in_specs=[pl.BlockSpec(memory_space=pltpu.SMEM)],
        out_specs=pl.BlockSpec((1, rows, cols), lambda i: (i, 0, 0)),
        out_shape=jax.ShapeDtypeStruct((n_hm, rows, cols), F32),
        compiler_params=_cparams(("arbitrary",)),
        name="rel_bias",
    )(tab_flat)


def _lambda(lamv_ref, lam_init):
    lv = lamv_ref[...]
    s1 = jnp.sum(lv[0:1] * lv[1:2], axis=1, keepdims=True)
    s2 = jnp.sum(lv[2:3] * lv[3:4], axis=1, keepdims=True)
    return jnp.exp(s1) - jnp.exp(s2) + lam_init


def _attn_prompt_kernel(far_ref, lamv_ref, q_ref, k_ref, v_ref, bd_ref, bs_ref, g_ref, o_ref,
                        m_scr, l_scr, acc_scr, *, t, half, lam_init):
    h = pl.program_id(1)
    i = pl.program_id(2)
    q = q_ref[0]
    lane = lax.broadcasted_iota(jnp.int32, q.shape, 1)
    zero = jnp.zeros(q.shape, q.dtype)
    qm = (jnp.where(lane < half, q, zero), jnp.where(lane >= half, q, zero))

    m_scr[...] = jnp.full(m_scr.shape, MASKED, F32)
    l_scr[...] = jnp.zeros(l_scr.shape, F32)
    acc_scr[...] = jnp.zeros(acc_scr.shape, F32)

    def block(j, bias):
        start = pl.multiple_of(j * t, t)
        kb = k_ref[0, pl.ds(start, t), :]
        vb = v_ref[0, pl.ds(start, t), :]
        for m in range(2):
            s = _dot_nt(qm[m], kb) + bias(m)
            m_old = m_scr[m]
            m_new = jnp.maximum(m_old, jnp.max(s, axis=1, keepdims=True))
            alpha = jnp.exp(m_old - m_new)
            p = jnp.exp(s - m_new)
            l_scr[m] = alpha * l_scr[m] + jnp.sum(p, axis=1, keepdims=True)
            acc_scr[m] = alpha * acc_scr[m] + _dot(p.astype(BF16), vb)
            m_scr[m] = m_new

    def far_body(j, carry):
        block(j, lambda m: far_ref[h * 2 + m])
        return carry

    lax.fori_loop(0, i - 1, far_body, 0)

    @pl.when(i >= 1)
    def _():
        block(i - 1, lambda m: bs_ref[0, m])

    block(i, lambda m: bd_ref[0, m])

    lam = _lambda(lamv_ref, lam_init)
    o = acc_scr[0] / l_scr[0] - lam * (acc_scr[1] / l_scr[1])
    ms = jnp.mean(o * o, axis=1, keepdims=True)
    o_ref[0] = (o * lax.rsqrt(ms + RMS_EPS) * g_ref[...] * (1.0 - lam_init)).astype(BF16)


def _attn_prompt_call(q16, k16, v16, far, lamv, bias_diag, bias_sub, subln_g, *, heads, t, lam_init):
    bsz, seq, w = q16.shape
    e = w // heads
    kern = functools.partial(_attn_prompt_kernel, t=t, half=e // 2, lam_init=lam_init)
    return pl.pallas_call(
        kern,
        grid=(bsz, heads, seq // t),
        in_specs=[pl.BlockSpec(memory_space=pltpu.SMEM),
                  pl.BlockSpec(lamv.shape, lambda b, h, i: (0, 0)),
                  pl.BlockSpec((1, t, e), lambda b, h, i: (b, i, h)),
                  pl.BlockSpec((1, seq, e), lambda b, h, i: (b, 0, h)),
                  pl.BlockSpec((1, seq, e), lambda b, h, i: (b, 0, h)),
                  pl.BlockSpec((1, 2, t, t), lambda b, h, i: (h, 0, 0, 0)),
                  pl.BlockSpec((1, 2, t, t), lambda b, h, i: (h, 0, 0, 0)),
                  pl.BlockSpec((1, e), lambda b, h, i: (0, 0))],
        out_specs=pl.BlockSpec((1, t, e), lambda b, h, i: (b, i, h)),
        out_shape=jax.ShapeDtypeStruct((bsz, seq, w), BF16),
        scratch_shapes=[pltpu.VMEM((2, t, 1), F32), pltpu.VMEM((2, t, 1), F32), pltpu.VMEM((2, t, e), F32)],
        compiler_params=_cparams(("parallel", "parallel", "arbitrary")),
        name="attn_prompt",
    )(far, lamv, q16, k16, v16, bias_diag, bias_sub, subln_g.reshape(1, e))


def _attn_sample_kernel(pt_ref, lamv_ref, qbd_ref, kc_ref, vc_ref, kn_ref, vn_ref, bfar_ref, blast_ref,
                        bnew_ref, g_ref, o_ref, m_scr, l_scr, acc_scr, *, n_pages, heads, t_new, lam_init):
    p = pl.program_id(1)
    qbd = qbd_ref[0]
    n_rows = qbd.shape[0]

    @pl.when(p == 0)
    def _():
        m_scr[...] = jnp.full(m_scr.shape, MASKED, F32)
        l_scr[...] = jnp.zeros(l_scr.shape, F32)
        acc_scr[...] = jnp.zeros(acc_scr.shape, F32)

    def update(s, vals):
        m_old = m_scr[...]
        m_new = jnp.maximum(m_old, jnp.max(s, axis=1, keepdims=True))
        alpha = jnp.exp(m_old - m_new)
        pr = jnp.exp(s - m_new)
        l_scr[...] = alpha * l_scr[...] + jnp.sum(pr, axis=1, keepdims=True)
        acc_scr[...] = alpha * acc_scr[...] + _dot(pr.astype(BF16), vals)
        m_scr[...] = m_new

    s = _dot_nt(qbd, kc_ref[0].astype(BF16))
    s = s + jnp.where(p == n_pages - 1, blast_ref[...], bfar_ref[...])
    update(s, vc_ref[0].astype(BF16))

    @pl.when(p == n_pages - 1)
    def _():
        update(_dot_nt(qbd, kn_ref[0]) + bnew_ref[...], vn_ref[0])
        w = acc_scr.shape[1]
        rowi = lax.broadcasted_iota(jnp.int32, (n_rows, w), 0)
        lanei = lax.broadcasted_iota(jnp.int32, (n_rows, w), 1)
        own = (lanei // LANES) == (rowi % heads)
        accn = jnp.where(own, acc_scr[...] / l_scr[...], 0.0)
        lam = _lambda(lamv_ref, lam_init)
        o_ref[...] = jnp.zeros(o_ref.shape, o_ref.dtype)
        for tq in range(t_new):
            o1 = jnp.sum(accn[tq * heads:(tq + 1) * heads], axis=0, keepdims=True)
            o2 = jnp.sum(accn[(t_new + tq) * heads:(t_new + tq + 1) * heads], axis=0, keepdims=True)
            o = o1 - lam * o2
            for hd in range(heads):
                seg = o[:, hd * LANES:(hd + 1) * LANES]
                ms = jnp.mean(seg * seg, axis=1, keepdims=True)
                o_ref[0, tq:tq + 1, hd * LANES:(hd + 1) * LANES] = (
                    seg * lax.rsqrt(ms + RMS_EPS) * g_ref[...] * (1.0 - lam_init)).astype(o_ref.dtype)


def _attn_sample_call(page_table, lamv, qbd, cache_k, cache_v, k_new, v_new, b_far, b_last, b_new, subln_g,
                      *, heads, t_new, lam_init):
    bsz, n_rows, w = qbd.shape
    n_pages = page_table.shape[1]
    page = cache_k.shape[1]
    rows_new = k_new.shape[1]
    kern = functools.partial(_attn_sample_kernel, n_pages=n_pages, heads=heads, t_new=t_new, lam_init=lam_init)
    grid_spec = pltpu.PrefetchScalarGridSpec(
        num_scalar_prefetch=1,
        grid=(bsz, n_pages),
        in_specs=[pl.BlockSpec(lamv.shape, lambda b, p, pt: (0, 0)),
                  pl.BlockSpec((1, n_rows, w), lambda b, p, pt: (b, 0, 0)),
                  pl.BlockSpec((1, page, w), lambda b, p, pt: (pt[b * n_pages + p], 0, 0)),
                  pl.BlockSpec((1, page, w), lambda b, p, pt: (pt[b * n_pages + p], 0, 0)),
                  pl.BlockSpec((1, rows_new, w), lambda b, p, pt: (b, 0, 0)),
                  pl.BlockSpec((1, rows_new, w), lambda b, p, pt: (b, 0, 0)),
                  pl.BlockSpec(b_far.shape, lambda b, p, pt: (0, 0)),
                  pl.BlockSpec(b_last.shape, lambda b, p, pt: (0, 0)),
                  pl.BlockSpec(b_new.shape, lambda b, p, pt: (0, 0)),
                  pl.BlockSpec((1, LANES), lambda b, p, pt: (0, 0))],
        out_specs=pl.BlockSpec((1, rows_new, w), lambda b, p, pt: (b, 0, 0)),
        scratch_shapes=[pltpu.VMEM((n_rows, 1), F32), pltpu.VMEM((n_rows, 1), F32), pltpu.VMEM((n_rows, w), F32)],
    )
    return pl.pallas_call(
        kern,
        grid_spec=grid_spec,
        out_shape=jax.ShapeDtypeStruct((bsz, rows_new, w), F32),
        compiler_params=_cparams(("parallel", "arbitrary")),
        name="attn_sample",
    )(page_table.reshape(-1), lamv, qbd, cache_k, cache_v, k_new, v_new, b_far, b_last, b_new,
      subln_g.reshape(1, LANES))


def _merge_kernel(x_ref, gt_ref, oa_ref, ob_ref, sga_ref, sgb_ref, wpa_ref, wpb_ref, wo_ref, lg_ref, lb_ref,
                  o_ref, *, alpha):
    m = (sga_ref[...].astype(F32) * _dot(oa_ref[...], wpa_ref[...])
         + sgb_ref[...].astype(F32) * _dot(ob_ref[...], wpb_ref[...]))
    r = _dot(m.astype(BF16), wo_ref[...])
    o_ref[...] = _layer_norm(alpha * x_ref[...] + gt_ref[...] * r, lg_ref[...], lb_ref[...])


def _merge_call(x, ada4, oa, ob, sga, sgb, w_pa, w_pb, w_o, ln_g, ln_b, *, alpha, tm, tiles_per_seq):
    n, d = x.shape
    row = pl.BlockSpec((tm, d), lambda i: (i, 0))
    wsp = pl.BlockSpec((d, d), lambda i: (0, 0))
    vec = pl.BlockSpec((1, d), lambda i: (0, 0))
    return pl.pallas_call(
        functools.partial(_merge_kernel, alpha=alpha),
        grid=(n // tm,),
        in_specs=[row, _ada_spec(ada4, 5, tm, tiles_per_seq), row, row, row, row, wsp, wsp, wsp, vec, vec],
        out_specs=row,
        out_shape=jax.ShapeDtypeStruct((n, d), F32),
        compiler_params=_cparams(("parallel",)),
        name="merge",
    )(x, ada4, oa, ob, sga, sgb, w_pa, w_pb, w_o, ln_g.reshape(1, d), ln_b.reshape(1, d))


def _row_tile(n, pref):
    tm = min(n, pref)
    while n % tm:
        tm //= 2
    return tm


def kernel(x_prompt, x_sample, cache_k, cache_v, state_hgrn, page_table, c_prompt, c_sample, w_ada, b_ada, ln_g, ln_b, ffn_w_in, ffn_w_out, w_in, hg_lower_bound, hg_norm_g, da_lambda_q1, da_lambda_k1, da_lambda_q2, da_lambda_k2, da_subln_g, rel_bias, w_pa, w_pb, w_o):
    depth = w_in.shape[0]
    bp, seq, d = x_prompt.shape
    bs, t_new, _ = x_sample.shape
    n_pool, page, da_heads, da_w = cache_k.shape[1:]
    hg_heads, hg_key = state_hgrn.shape[2], state_hgrn.shape[3]
    n_pages = page_table.shape[1]
    past = n_pages * page
    alpha = (2.0 * depth) ** 0.25
    t_attn = min(seq, 256)
    assert t_attn >= REL_MAX_DIST - 1 and page >= REL_MAX_DIST - 1
    assert da_w == LANES and hg_key == LANES and d == hg_heads * LANES == da_heads * LANES

    n_hm = da_heads * 2
    tab = jnp.transpose(rel_bias.astype(F32), (2, 1, 0)).reshape(-1)
    bias_diag = _relbias_call(tab, n_hm, t_attn, t_attn, 0).reshape(da_heads, 2, t_attn, t_attn)
    bias_sub = _relbias_call(tab, n_hm, t_attn, t_attn, t_attn).reshape(da_heads, 2, t_attn, t_attn)
    far_tile = _relbias_call(tab, n_hm, 8, LANES, 2 * REL_MAX_DIST + LANES)
    far = far_tile[:, 0, 0]
    rows_new = 16
    last_tile = _relbias_call(tab, n_hm, 8, LANES, page)
    new_tile = _relbias_call(tab, n_hm, 8, LANES, 0)

    def sample_rows(tile, cols):
        a = tile.reshape(da_heads, 2, 8, LANES)[:, :, :t_new, :cols]
        return jnp.transpose(a, (1, 2, 0, 3)).reshape(2 * t_new * da_heads, cols)

    sb_far = sample_rows(far_tile, 1)
    sb_last = sample_rows(last_tile, LANES)
    sb_new = sample_rows(new_tile, rows_new)

    n_p, n_s = bp * seq, bs * t_new
    tm_p = _row_tile(seq, 512)
    tps_p = seq // tm_p
    tm_s = _row_tile(n_s, 512)
    hg_chunk = min(seq, 64)
    t_pad = 16
    row_h = jnp.arange(2 * t_new * da_heads) % da_heads
    row_m = jnp.arange(2 * t_new * da_heads) // (t_new * da_heads)
    lane_h = jnp.arange(d) // LANES
    lane_m = (jnp.arange(d) // (LANES // 2)) % 2
    bd_mask = (row_h[:, None] == lane_h[None, :]) & (row_m[:, None] == lane_m[None, :])

    y_p = x_prompt.reshape(n_p, d)
    y_s = x_sample.reshape(n_s, d)
    outs = [[] for _ in range(6)]
    for l in range(depth):
        lam_init = 0.8 - 0.6 * math.exp(-0.3 * l)
        lamv = jnp.stack([da_lambda_q1[l], da_lambda_k1[l], da_lambda_q2[l], da_lambda_k2[l]]).astype(F32)
        ada = _ada_call(jnp.concatenate([c_prompt, c_sample]).astype(F32), w_ada[l], b_ada[l])
        ada_p = jnp.transpose(ada[:bp].reshape(bp, 9, d), (1, 0, 2)).reshape(9, bp, 1, d)
        ada_s = jnp.transpose(jnp.repeat(ada[bp:], t_new, axis=0).reshape(n_s, 9, d), (1, 0, 2)).reshape(9, 1, n_s, d)
        wf_in = ffn_w_in[l].astype(BF16)
        wf_out = ffn_w_out[l].astype(BF16)
        w_in_l = w_in[l].astype(BF16)
        wpa, wpb, wo = w_pa[l].astype(BF16), w_pb[l].astype(BF16), w_o[l].astype(BF16)

        def trunk(y, ada4, tm, tps, attend, hgrn):
            x1 = _ffn_ln_call(y, ada4, 0, wf_in[0], wf_out[0], ln_g[l, 0], ln_b[l, 0],
                              alpha=alpha, tm=tm, tiles_per_seq=tps)
            qa, lf, ka, va, ga, q16, kb, k16, vb, v16, sga, sgb = _inproj_call(
                x1, ada4, w_in_l, hg_lower_bound.astype(F32), layer=l, hg_key=hg_key,
                da_head_dim=da_w // 2, tm=tm, tiles_per_seq=tps)
            oa, s_new = hgrn(qa, lf, ka, va, ga)
            ob = attend(q16, k16, v16)
            x2 = _merge_call(x1, ada4, oa, ob, sga, sgb, wpa, wpb, wo, ln_g[l, 1], ln_b[l, 1],
                             alpha=alpha, tm=tm, tiles_per_seq=tps)
            y_new = _ffn_ln_call(x2, ada4, 2, wf_in[1], wf_out[1], ln_g[l, 2], ln_b[l, 2],
                                 alpha=alpha, tm=tm, tiles_per_seq=tps)
            return y_new, kb, vb, s_new

        def hgrn_p(qa, lf, ka, va, ga):
            r3 = lambda a: a.reshape(bp, seq, d)
            oa, s_new = _hgrn_call(r3(qa), r3(lf), r3(ka), r3(va), r3(ga), hg_norm_g[l], None, chunk=hg_chunk)
            return oa.reshape(n_p, d), s_new

        def attend_p(q16, k16, v16):
            r3 = lambda a: a.reshape(bp, seq, d)
            ob = _attn_prompt_call(r3(q16), r3(k16), r3(v16), far, lamv, bias_diag, bias_sub, da_subln_g[l],
                                   heads=da_heads, t=t_attn, lam_init=lam_init)
            return ob.reshape(n_p, d)

        def hgrn_s(qa, lf, ka, va, ga):
            pad = lambda a: jnp.pad(a.reshape(bs, t_new, d), ((0, 0), (0, t_pad - t_new), (0, 0)))
            oa, s_new = _hgrn_call(pad(qa), pad(lf), pad(ka), pad(va), pad(ga), hg_norm_g[l],
                                   state_hgrn[l].astype(F32), chunk=t_pad)
            return oa[:, :t_new].reshape(n_s, d), s_new

        def attend_s(q16, k16, v16):
            q3 = q16.reshape(bs, t_new, d)
            qrep = jnp.broadcast_to(q3[:, None, :, None, :], (bs, 2, t_new, da_heads, d))
            qbd = jnp.where(bd_mask[None], qrep.reshape(bs, 2 * t_new * da_heads, d), jnp.zeros((), BF16))
            padn = lambda a: jnp.pad(a.reshape(bs, t_new, d), ((0, 0), (0, rows_new - t_new), (0, 0)))
            ob = _attn_sample_call(page_table.astype(jnp.int32), lamv, qbd,
                                   cache_k[l].reshape(n_pool, page, d), cache_v[l].reshape(n_pool, page, d),
                                   padn(k16), padn(v16), sb_far, sb_last, sb_new, da_subln_g[l],
                                   heads=da_heads, t_new=t_new, lam_init=lam_init)
            return ob[:, :t_new].reshape(n_s, d).astype(BF16)

        y_p, k_p, v_p, s_p = trunk(y_p, ada_p, tm_p, tps_p, attend_p, hgrn_p)
        y_s, k_s, v_s, s_s = trunk(y_s, ada_s, tm_s, 1, attend_s, hgrn_s)
        for lst, val in zip(outs, (k_p.reshape(bp, seq, da_heads, da_w), v_p.reshape(bp, seq, da_heads, da_w), s_p,
                                   k_s.reshape(bs, t_new, da_heads, da_w), v_s.reshape(bs, t_new, da_heads, da_w), s_s)):
            lst.append(val)

    k_prompt, v_prompt, st_prompt, k_sample, v_sample, st_sample = (jnp.stack(o) for o in outs)
    return (y_p.reshape(bp, seq, d), y_s.reshape(bs, t_new, d), k_prompt, v_prompt, st_prompt,
            k_sample, v_sample, st_sample)
```

```python
import functools
import math

import jax
import jax.numpy as jnp
from jax import lax
from jax.experimental import pallas as pl
from jax.experimental.pallas import tpu as pltpu

F32 = jnp.float32
BF16 = jnp.bfloat16

LN_EPS = 1e-5
RMS_EPS = 1e-5
REL_BUCKETS = 32
REL_MAX_DIST = 128
MASKED = -1e30
LOG2E = math.log2(math.e)
VMEM_LIMIT_BYTES = 56 * 1024 * 1024
LANES = 128
BF16_ROWS = 16


def _cparams(sem):
    return pltpu.CompilerParams(dimension_semantics=sem, vmem_limit_bytes=VMEM_LIMIT_BYTES)


def _sigmoid(x):
    return 1.0 / (1.0 + jnp.exp(-x))


def _dot(a, b):
    return jnp.dot(a, b, preferred_element_type=F32)


def _dot_nt(a, b):
    return lax.dot_general(a, b, (((1,), (1,)), ((), ())), preferred_element_type=F32)


def _dot_tn(a, b):
    return lax.dot_general(a, b, (((0,), (0,)), ((), ())), preferred_element_type=F32)


def _layer_norm(x, g, b):
    mu = jnp.mean(x, axis=-1, keepdims=True)
    xc = x - mu
    var = jnp.mean(xc * xc, axis=-1, keepdims=True)
    return xc * lax.rsqrt(var + LN_EPS) * g + b


def _ada_kernel(c_ref, w_ref, b_ref, o_ref):
    c = c_ref[...]
    s = c * _sigmoid(c)
    o_ref[...] = jnp.dot(s, w_ref[...], preferred_element_type=F32,
                         precision=lax.Precision.HIGHEST) + b_ref[...]


def _ada_call(c, w, b):
    rows, d = c.shape
    width = w.shape[1]
    tn = 1152 if width % 1152 == 0 else width
    return pl.pallas_call(
        _ada_kernel,
        grid=(width // tn,),
        in_specs=[pl.BlockSpec((rows, d), lambda j: (0, 0)),
                  pl.BlockSpec((d, tn), lambda j: (0, j)),
                  pl.BlockSpec((1, tn), lambda j: (0, j))],
        out_specs=pl.BlockSpec((rows, tn), lambda j: (0, j)),
        out_shape=jax.ShapeDtypeStruct((rows, width), F32),
        compiler_params=_cparams(("arbitrary",)),
        name="ada_proj",
    )(c, w, b.reshape(1, width))


def _ada_spec(ada4, k, tm, tiles_per_seq):
    d = ada4.shape[-1]
    if ada4.shape[2] == 1:
        return pl.BlockSpec((None, None, 1, d), lambda i, *_: (k, i // tiles_per_seq, 0, 0))
    return pl.BlockSpec((None, None, tm, d), lambda i, *_: (k, 0, i, 0))


def _ffn_ln_kernel(x_ref, sh_ref, sc_ref, gt_ref, wg_ref, wu_ref, wo_ref, lg_ref, lb_ref, o_ref, *, alpha):
    x = x_ref[...]
    h = (x * (1.0 + sc_ref[...]) + sh_ref[...]).astype(BF16)
    gate = _dot(h, wg_ref[...])
    up = _dot(h, wu_ref[...])
    a = (gate * _sigmoid(gate) * up).astype(BF16)
    y = alpha * x + 0.5 * gt_ref[...] * _dot(a, wo_ref[...])
    o_ref[...] = _layer_norm(y, lg_ref[...], lb_ref[...])


def _resident(shape, index_map):
    return pl.BlockSpec(shape, index_map, pipeline_mode=pl.Buffered(1))


def _ffn_ln_call(x, ada4, sub, w_in, w_out, ln_g, ln_b, *, alpha, tm, tiles_per_seq):
    n, d = x.shape
    f = w_out.shape[0]
    return pl.pallas_call(
        functools.partial(_ffn_ln_kernel, alpha=alpha),
        grid=(n // tm,),
        in_specs=[pl.BlockSpec((tm, d), lambda i: (i, 0)),
                  _ada_spec(ada4, 3 * sub + 0, tm, tiles_per_seq),
                  _ada_spec(ada4, 3 * sub + 1, tm, tiles_per_seq),
                  _ada_spec(ada4, 3 * sub + 2, tm, tiles_per_seq),
                  _resident((d, f), lambda i: (0, 0)),
                  _resident((d, f), lambda i: (0, 1)),
                  _resident((f, d), lambda i: (0, 0)),
                  pl.BlockSpec((1, d), lambda i: (0, 0)),
                  pl.BlockSpec((1, d), lambda i: (0, 0))],
        out_specs=pl.BlockSpec((tm, d), lambda i: (i, 0)),
        out_shape=jax.ShapeDtypeStruct((n, d), F32),
        compiler_params=_cparams(("parallel",)),
        name="ffn_ln",
    )(x, ada4, ada4, ada4, w_in, w_in, w_out, ln_g.reshape(1, d), ln_b.reshape(1, d))


def _inproj_kernel(x_ref, sh_ref, sc_ref, w_ref, lbp_ref,
                   qa_ref, lf_ref, ka_ref, va_ref, ga_ref, q16_ref, kb_ref, k16_ref, vb_ref, v16_ref,
                   sga_ref, sgb_ref, h_scr, *, layer, hg_scale, da_scale, row_chunk):
    j = pl.program_id(1)

    @pl.when(j == 0)
    def _():
        h_scr[...] = (x_ref[...] * (1.0 + sc_ref[...]) + sh_ref[...]).astype(BF16)

    def column_block(jj, emit):
        @pl.when(j == jj)
        def _():
            for r0 in range(0, h_scr.shape[0], row_chunk):
                rs = slice(r0, r0 + row_chunk)
                emit(rs, _dot(h_scr[rs], w_ref[...]))

    def emit_qa(rs, z):
        qa_ref[rs] = (z * _sigmoid(z) * hg_scale).astype(BF16)

    def emit_forget(rs, z):
        p = lbp_ref[...]
        e = jnp.exp(p - jnp.max(p, axis=0, keepdims=True))
        lb = jnp.sum(e[:layer + 1], axis=0, keepdims=True) / jnp.sum(e, axis=0, keepdims=True)
        f = lb + (1.0 - lb) * _sigmoid(z)
        lf_ref[rs] = jnp.log(f)
        ka_ref[rs] = (1.0 - f).astype(BF16)

    def emit_va(rs, z):
        va_ref[rs] = z.astype(BF16)

    def emit_ga(rs, z):
        ga_ref[rs] = (z * _sigmoid(z)).astype(BF16)

    def emit_q(rs, z):
        q16_ref[rs] = (z * da_scale).astype(BF16)

    def emit_k(rs, z):
        kb_ref[rs] = z
        k16_ref[rs] = z.astype(BF16)

    def emit_v(rs, z):
        vb_ref[rs] = z
        v16_ref[rs] = z.astype(BF16)

    def emit_sga(rs, z):
        sga_ref[rs] = _sigmoid(z).astype(BF16)

    def emit_sgb(rs, z):
        sgb_ref[rs] = _sigmoid(z).astype(BF16)

    for jj, emit in enumerate((emit_qa, emit_forget, emit_va, emit_ga, emit_q, emit_k, emit_v, emit_sga, emit_sgb)):
        column_block(jj, emit)


def _inproj_call(x, ada4, w_in, lb_params, *, layer, hg_key, da_head_dim, tm, tiles_per_seq):
    n, d = x.shape
    nj = w_in.shape[1] // d
    assert nj == 9
    kern = functools.partial(_inproj_kernel, layer=layer, hg_scale=hg_key ** -0.5,
                             da_scale=da_head_dim ** -0.5 * LOG2E, row_chunk=min(tm, 128))
    out_dtypes = [BF16, F32, BF16, BF16, BF16, BF16, F32, BF16, F32, BF16, BF16, BF16]
    row_spec = pl.BlockSpec((tm, d), lambda i, j: (i, 0))
    return pl.pallas_call(
        kern,
        grid=(n // tm, nj),
        in_specs=[row_spec,
                  _ada_spec(ada4, 3, tm, tiles_per_seq),
                  _ada_spec(ada4, 4, tm, tiles_per_seq),
                  pl.BlockSpec((d, d), lambda i, j: (0, j)),
                  pl.BlockSpec(lb_params.shape, lambda i, j: (0, 0))],
        out_specs=[row_spec] * len(out_dtypes),
        out_shape=[jax.ShapeDtypeStruct((n, d), dt) for dt in out_dtypes],
        scratch_shapes=[pltpu.VMEM((tm, d), BF16)],
        compiler_params=_cparams(("parallel", "arbitrary")),
        name="in_proj",
    )(x, ada4, ada4, w_in, lb_params)


def _hgrn_kernel(*refs, chunk, heads, has_s0):
    if has_s0:
        q_ref, g_ref, k_ref, v_ref, gt_ref, ng_ref, s0_ref, o_ref, s_ref = refs
    else:
        q_ref, g_ref, k_ref, v_ref, gt_ref, ng_ref, o_ref, s_ref = refs
        s0_ref = None
    C = chunk
    W = g_ref.shape[-1]

    @pl.when(pl.program_id(1) == 0)
    def _():
        if has_s0:
            s_ref[...] = s0_ref[...]
        else:
            s_ref[...] = jnp.zeros(s_ref.shape, F32)

    g = g_ref[0]
    row = lax.broadcasted_iota(jnp.int32, (C, W), 0)
    b = g
    sh = 1
    while sh < C:
        b = b + jnp.where(row >= sh, pltpu.roll(b, sh, 0), 0.0)
        sh *= 2

    ti = lax.broadcasted_iota(jnp.int32, (C, C), 0)
    si = lax.broadcasted_iota(jnp.int32, (C, C), 1)
    levels = []
    u = b
    hh = 1
    while hh < C:
        right = (row & (2 * hh - 1)) >= hh
        ref = jnp.where(right, pltpu.roll(u, hh, 0), u)
        e = jnp.exp(-jnp.abs(b - ref))
        pair = (((ti ^ si) & ~(2 * hh - 1)) == 0) & ((ti & (2 * hh - 1)) >= hh) & ((si & (2 * hh - 1)) < hh)
        levels.append((e, pair))
        u = jnp.where(right, u, pltpu.roll(u, C - hh, 0))
        hh *= 2
    b_last = u

    q = q_ref[0].astype(F32)
    k = k_ref[0].astype(F32)
    v = v_ref[0]
    gt = gt_ref[0].astype(F32)
    q_in = (q * jnp.exp(b)).astype(BF16)
    k_st = (k * jnp.exp(b_last - b)).astype(BF16)
    qk = q * k
    q_lv = [(q * e).astype(BF16) for e, _ in levels]
    k_lv = [(k * e).astype(BF16) for e, _ in levels]
    eye = lax.broadcasted_iota(jnp.int32, (LANES, LANES), 0) == lax.broadcasted_iota(jnp.int32, (LANES, LANES), 1)
    ng = ng_ref[...]

    for h in range(heads):
        sl = slice(h * LANES, (h + 1) * LANES)
        a = jnp.zeros((C, C), F32)
        for lv, (_, pair) in enumerate(levels):
            a = a + jnp.where(pair, _dot_nt(q_lv[lv][:, sl], k_lv[lv][:, sl]), 0.0)
        s_old = s_ref[0, h]
        vh = v[:, sl]
        diag = jnp.sum(qk[:, sl], axis=1, keepdims=True)
        o = _dot(q_in[:, sl], s_old.astype(BF16)) + _dot(a.astype(BF16), vh) + diag * vh.astype(F32)
        ms = jnp.mean(o * o, axis=1, keepdims=True)
        o_ref[0, :, sl] = (o * lax.rsqrt(ms + RMS_EPS) * ng * gt[:, sl]).astype(BF16)
        bl_col = jnp.sum(jnp.where(eye, jnp.broadcast_to(b_last[0:1, sl], (LANES, LANES)), 0.0),
                         axis=1, keepdims=True)
        s_ref[0, h] = jnp.exp(bl_col) * s_old + _dot_tn(k_st[:, sl], vh)


def _hgrn_call(qa, lf, ka, va, ga, norm_g, s0, *, chunk):
    bsz, t, w = qa.shape
    heads = w // LANES
    has_s0 = s0 is not None
    kern = functools.partial(_hgrn_kernel, chunk=chunk, heads=heads, has_s0=has_s0)
    tok = pl.BlockSpec((1, chunk, w), lambda b, c: (b, c, 0))
    st = pl.BlockSpec((1, heads, LANES, LANES), lambda b, c: (b, 0, 0, 0))
    in_specs = [tok, tok, tok, tok, tok, pl.BlockSpec((1, LANES), lambda b, c: (0, 0))]
    args = [qa, lf, ka, va, ga, norm_g.reshape(1, LANES)]
    if has_s0:
        in_specs.append(st)
        args.append(s0)
    return pl.pallas_call(
        kern,
        grid=(bsz, t // chunk),
        in_specs=in_specs,
        out_specs=[tok, st],
        out_shape=[jax.ShapeDtypeStruct((bsz, t, w), BF16),
                   jax.ShapeDtypeStruct((bsz, heads, LANES, LANES), F32)],
        compiler_params=_cparams(("parallel", "arbitrary")),
        name="hgrn2",
    )(*args)


def _relbias_kernel(tab_ref, o_ref, *, base, stride):
    hm = pl.program_id(0)
    rows, cols = o_ref.shape
    off = base + pl.program_id(1) * stride
    n = off + lax.broadcasted_iota(jnp.int32, (rows, cols), 0) - lax.broadcasted_iota(jnp.int32, (rows, cols), 1)
    nn = jnp.maximum(n, 0)
    max_exact = REL_BUCKETS // 2
    nf = jnp.maximum(nn, 1).astype(F32)
    large = max_exact + (jnp.log(nf / max_exact) / math.log(REL_MAX_DIST / max_exact)
                         * (REL_BUCKETS - max_exact)).astype(jnp.int32)
    bucket = jnp.where(nn < max_exact, nn, jnp.minimum(large, REL_BUCKETS - 1))
    val = jnp.zeros((rows, cols), F32)
    for bk in range(REL_BUCKETS):
        val = jnp.where(bucket == bk, tab_ref[hm * REL_BUCKETS + bk], val)
    o_ref[...] = jnp.where(n < 0, MASKED, val * LOG2E)


def _relbias_call(tab_flat, heads, kinds, rows, cols, base, stride):
    return pl.pallas_call(
        functools.partial(_relbias_kernel, base=base, stride=stride),
        grid=(heads * 2, kinds),
        in_specs=[pl.BlockSpec(memory_space=pltpu.SMEM)],
        out_specs=pl.BlockSpec((None, None, None, rows, cols), lambda i, kd: (i // 2, kd, i % 2, 0, 0)),
        out_shape=jax.ShapeDtypeStruct((heads, kinds, 2, rows, cols), F32),
        compiler_params=_cparams(("arbitrary", "arbitrary")),
        name="rel_bias",
    )(tab_flat)


def _lambda(lamv_ref, lam_init):
    lv = lamv_ref[...]
    s1 = jnp.sum(lv[0:1] * lv[1:2], axis=1, keepdims=True)
    s2 = jnp.sum(lv[2:3] * lv[3:4], axis=1, keepdims=True)
    return jnp.exp(s1) - jnp.exp(s2) + lam_init


def _attn_prompt_kernel(lamv_ref, q_ref, k_ref, v_ref, bias_ref, g_ref, o_ref,
                        q2_scr, s_scr, mx_scr, ls_scr, acc_scr, *, t, half, lam_init):
    i = pl.program_id(2)
    q = q_ref[0]
    lane = lax.broadcasted_iota(jnp.int32, q.shape, 1)
    zero = jnp.zeros(q.shape, q.dtype)
    q2_scr[0:t] = jnp.where(lane < half, q, zero)
    q2_scr[t:2 * t] = jnp.where(lane >= half, q, zero)
    mx_scr[...] = jnp.full(mx_scr.shape, MASKED, F32)
    n_sub = t // LANES

    def logits(j, carry):
        start = pl.multiple_of(j * t, t)
        s = _dot_nt(q2_scr[...], k_ref[0, pl.ds(start, t), :]) + bias_ref[0, jnp.minimum(i - j, 2)]
        s_scr[j] = s
        mx = mx_scr[...]
        for u in range(n_sub):
            mx = jnp.maximum(mx, s[:, u * LANES:(u + 1) * LANES])
        mx_scr[...] = mx
        return carry

    lax.fori_loop(0, i + 1, logits, 0)

    m = jnp.max(mx_scr[...], axis=1, keepdims=True)
    mx_scr[...] = jnp.broadcast_to(m, mx_scr.shape)
    ls_scr[...] = jnp.zeros(ls_scr.shape, F32)
    acc_scr[...] = jnp.zeros(acc_scr.shape, F32)

    def weighted(j, carry):
        start = pl.multiple_of(j * t, t)
        s = s_scr[j]
        mb = mx_scr[...]
        ps = [jnp.exp2(s[:, u * LANES:(u + 1) * LANES] - mb) for u in range(n_sub)]
        ls = ls_scr[...]
        for pu in ps:
            ls = ls + pu
        ls_scr[...] = ls
        p = jnp.concatenate([pu.astype(BF16) for pu in ps], axis=1)
        acc_scr[...] += _dot(p, v_ref[0, pl.ds(start, t), :])
        return carry

    lax.fori_loop(0, i + 1, weighted, 0)

    on = acc_scr[...] / jnp.sum(ls_scr[...], axis=1, keepdims=True)
    o = on[0:t] - _lambda(lamv_ref, lam_init) * on[t:2 * t]
    ms = jnp.mean(o * o, axis=1, keepdims=True)
    o_ref[0] = (o * lax.rsqrt(ms + RMS_EPS) * g_ref[...] * (1.0 - lam_init)).astype(BF16)


def _attn_prompt_call(q16, k16, v16, lamv, bias, subln_g, *, heads, t, lam_init):
    bsz, seq, w = q16.shape
    e = w // heads
    nq = seq // t
    kern = functools.partial(_attn_prompt_kernel, t=t, half=e // 2, lam_init=lam_init)
    return pl.pallas_call(
        kern,
        grid=(bsz, heads, nq),
        in_specs=[pl.BlockSpec(lamv.shape, lambda b, h, i: (0, 0)),
                  pl.BlockSpec((1, t, e), lambda b, h, i: (b, i, h)),
                  pl.BlockSpec((1, seq, e), lambda b, h, i: (b, 0, h)),
                  pl.BlockSpec((1, seq, e), lambda b, h, i: (b, 0, h)),
                  pl.BlockSpec((1, 3, 2 * t, t), lambda b, h, i: (h, 0, 0, 0)),
                  pl.BlockSpec((1, e), lambda b, h, i: (0, 0))],
        out_specs=pl.BlockSpec((1, t, e), lambda b, h, i: (b, i, h)),
        out_shape=jax.ShapeDtypeStruct((bsz, seq, w), BF16),
        scratch_shapes=[pltpu.VMEM((2 * t, e), BF16), pltpu.VMEM((nq, 2 * t, t), F32),
                        pltpu.VMEM((2 * t, LANES), F32), pltpu.VMEM((2 * t, LANES), F32),
                        pltpu.VMEM((2 * t, e), F32)],
        compiler_params=_cparams(("parallel", "parallel", "arbitrary")),
        name="attn_prompt",
    )(lamv, q16, k16, v16, bias, subln_g.reshape(1, e))


def _attn_sample_kernel(pt_ref, lamv_ref, qh_ref, *rest, group, n_steps, heads, t_new, lam_init):
    k_refs, v_refs = rest[:group], rest[group:2 * group]
    kn_ref, vn_ref, bfar_ref, blast_ref, bnew_ref, g_ref, o_ref, m_scr, l_scr, acc_scr = rest[2 * group:]
    p = pl.program_id(1)
    page = k_refs[0].shape[0] // heads
    rows = qh_ref.shape[2] // 2

    @pl.when(p == 0)
    def _():
        m_scr[...] = jnp.full(m_scr.shape, MASKED, F32)
        l_scr[...] = jnp.zeros(l_scr.shape, F32)
        acc_scr[...] = jnp.zeros(acc_scr.shape, F32)

    def update(c, s, vals):
        m_old = m_scr[c]
        m_new = jnp.maximum(m_old, jnp.max(s, axis=1, keepdims=True))
        alpha = jnp.exp2(m_old - m_new)
        pr = jnp.exp2(s - m_new)
        l_scr[c] = alpha * l_scr[c] + jnp.sum(pr, axis=1, keepdims=True)
        acc_scr[c] = alpha * acc_scr[c] + _dot(pr.astype(BF16), vals)
        m_scr[c] = m_new

    def pair_pages(refs, c):
        even, odd = pl.ds(2 * c, page, stride=heads), pl.ds(2 * c + 1, page, stride=heads)
        return jnp.concatenate([jnp.concatenate([r[even, :], r[odd, :]], axis=1) for r in refs],
                               axis=0).astype(BF16)

    last = p == n_steps - 1
    for c in range(heads // 2):
        s = _dot_nt(qh_ref[0, c], pair_pages(k_refs, c)) + jnp.where(last, blast_ref[c], bfar_ref[c])
        update(c, s, pair_pages(v_refs, c))

    @pl.when(last)
    def _():
        lam = _lambda(lamv_ref, lam_init)
        for c in range(heads // 2):
            sl2 = slice(2 * c * LANES, (2 * c + 2) * LANES)
            update(c, _dot_nt(qh_ref[0, c], kn_ref[0, :, sl2]) + bnew_ref[c], vn_ref[0, :, sl2])
            on2 = acc_scr[c] / l_scr[c]
            for side in range(2):
                on = on2[side * rows:(side + 1) * rows, side * LANES:(side + 1) * LANES]
                o = on - lam * pltpu.roll(on, rows - t_new, 0)
                ms = jnp.mean(o * o, axis=1, keepdims=True)
                sl = slice((2 * c + side) * LANES, (2 * c + side + 1) * LANES)
                o_ref[0, :, sl] = o * lax.rsqrt(ms + RMS_EPS) * g_ref[...] * (1.0 - lam_init)


def _attn_sample_call(page_table, lamv, qh, cache_k, cache_v, layer, k_new, v_new, b_far, b_last, b_new, subln_g,
                      *, group, t_new, lam_init):
    bsz, pairs, rows2, e2 = qh.shape
    heads, rows, e = 2 * pairs, rows2 // 2, e2 // 2
    n_pages = page_table.shape[1]
    n_steps = n_pages // group
    page_rows = cache_k.shape[2]
    w = heads * e
    rows_new = k_new.shape[1]
    kern = functools.partial(_attn_sample_kernel, group=group, n_steps=n_steps, heads=heads, t_new=t_new,
                             lam_init=lam_init)

    def page_spec(r):
        return pl.BlockSpec((None, None, page_rows, e),
                            lambda b, p, pt: (layer, pt[b * n_pages + p * group + r], 0, 0))

    fixed = lambda shape: pl.BlockSpec(shape, lambda b, p, pt: (0,) * len(shape))
    grid_spec = pltpu.PrefetchScalarGridSpec(
        num_scalar_prefetch=1,
        grid=(bsz, n_steps),
        in_specs=[fixed(lamv.shape),
                  pl.BlockSpec((1, pairs, rows2, e2), lambda b, p, pt: (b, 0, 0, 0))]
                 + [page_spec(r) for r in range(group)] * 2
                 + [pl.BlockSpec((1, rows_new, w), lambda b, p, pt: (b, 0, 0)),
                    pl.BlockSpec((1, rows_new, w), lambda b, p, pt: (b, 0, 0)),
                    fixed(b_far.shape), fixed(b_last.shape), fixed(b_new.shape), fixed((1, e))],
        out_specs=pl.BlockSpec((1, rows, w), lambda b, p, pt: (b, 0, 0)),
        scratch_shapes=[pltpu.VMEM((pairs, rows2, 1), F32), pltpu.VMEM((pairs, rows2, 1), F32),
                        pltpu.VMEM((pairs, rows2, e2), F32)],
    )
    return pl.pallas_call(
        kern,
        grid_spec=grid_spec,
        out_shape=jax.ShapeDtypeStruct((bsz, rows, w), F32),
        compiler_params=_cparams(("parallel", "arbitrary")),
        name="attn_sample",
    )(page_table.reshape(-1), lamv, qh, *([cache_k] * group), *([cache_v] * group), k_new, v_new,
      b_far, b_last, b_new, subln_g.reshape(1, e))


def _merge_kernel(x_ref, gt_ref, oa_ref, ob_ref, sga_ref, sgb_ref, wpa_ref, wpb_ref, wo_ref, lg_ref, lb_ref,
                  o_ref, *, alpha):
    m = (sga_ref[...].astype(F32) * _dot(oa_ref[...], wpa_ref[...])
         + sgb_ref[...].astype(F32) * _dot(ob_ref[...], wpb_ref[...]))
    r = _dot(m.astype(BF16), wo_ref[...])
    o_ref[...] = _layer_norm(alpha * x_ref[...] + gt_ref[...] * r, lg_ref[...], lb_ref[...])


def _merge_call(x, ada4, oa, ob, sga, sgb, w_pa, w_pb, w_o, ln_g, ln_b, *, alpha, tm, tiles_per_seq):
    n, d = x.shape
    row = pl.BlockSpec((tm, d), lambda i: (i, 0))
    wsp = pl.BlockSpec((d, d), lambda i: (0, 0))
    vec = pl.BlockSpec((1, d), lambda i: (0, 0))
    return pl.pallas_call(
        functools.partial(_merge_kernel, alpha=alpha),
        grid=(n // tm,),
        in_specs=[row, _ada_spec(ada4, 5, tm, tiles_per_seq), row, row, row, row, wsp, wsp, wsp, vec, vec],
        out_specs=row,
        out_shape=jax.ShapeDtypeStruct((n, d), F32),
        compiler_params=_cparams(("parallel",)),
        name="merge",
    )(x, ada4, oa, ob, sga, sgb, w_pa, w_pb, w_o, ln_g.reshape(1, d), ln_b.reshape(1, d))


def _row_tile(n, pref):
    tm = min(n, pref)
    while n % tm:
        tm //= 2
    return tm


def kernel(x_prompt, x_sample, cache_k, cache_v, state_hgrn, page_table, c_prompt, c_sample, w_ada, b_ada, ln_g, ln_b, ffn_w_in, ffn_w_out, w_in, hg_lower_bound, hg_norm_g, da_lambda_q1, da_lambda_k1, da_lambda_q2, da_lambda_k2, da_subln_g, rel_bias, w_pa, w_pb, w_o):
    depth = w_in.shape[0]
    bp, seq, d = x_prompt.shape
    bs, t_new, _ = x_sample.shape
    n_pool, page, da_heads, da_w = cache_k.shape[1:]
    hg_heads, hg_key = state_hgrn.shape[2], state_hgrn.shape[3]
    n_pages = page_table.shape[1]
    alpha = (2.0 * depth) ** 0.25
    t_attn = min(seq, 512)
    assert t_attn >= REL_MAX_DIST - 1 and page >= REL_MAX_DIST - 1
    assert da_w == LANES and hg_key == LANES and d == hg_heads * LANES == da_heads * LANES
    assert 2 * t_new <= BF16_ROWS

    group = 8
    while n_pages % group:
        group //= 2
    tab = jnp.transpose(rel_bias.astype(F32), (2, 1, 0)).reshape(-1)
    bias_p = _relbias_call(tab, da_heads, 3, t_attn, t_attn, 0, t_attn).reshape(da_heads, 3, 2 * t_attn, t_attn)
    far_tile = _relbias_call(tab, da_heads, 1, 8, LANES, 2 * REL_MAX_DIST + LANES, 0)
    last_tile = _relbias_call(tab, da_heads, 1, 8, LANES, page, 0)
    new_tile = _relbias_call(tab, da_heads, 1, 8, LANES, 0, 0)

    def sample_rows(tile, cols):
        a = tile[:, 0, :, :t_new, :cols].reshape(da_heads, 2 * t_new, cols)
        return jnp.pad(a, ((0, 0), (0, BF16_ROWS - 2 * t_new), (0, 0)))

    def pair_rows(a):
        return a.reshape(da_heads // 2, 2 * BF16_ROWS, a.shape[-1])

    sb_far = sample_rows(far_tile, 1)
    sb_last = jnp.concatenate([jnp.broadcast_to(sb_far, (da_heads, BF16_ROWS, (group - 1) * page)),
                               sample_rows(last_tile, page)], axis=2)
    sb_new = pair_rows(sample_rows(new_tile, BF16_ROWS))
    sb_far, sb_last = pair_rows(sb_far), pair_rows(sb_last)

    n_p, n_s = bp * seq, bs * t_new
    tm_p = _row_tile(seq, 512)
    tps_p = seq // tm_p
    tm_s = _row_tile(n_s, 512)
    hg_chunk = min(seq, 64)
    t_pad = BF16_ROWS
    map_of_lane = (jnp.arange(LANES) // (LANES // 2))[None, :]
    map_of_row = (jnp.arange(2 * t_new) // t_new)[:, None]
    own_map = (map_of_lane == map_of_row)[None, None]

    y_p = x_prompt.reshape(n_p, d)
    y_s = x_sample.reshape(n_s, d)
    cache_k2 = cache_k.reshape(depth, n_pool, page * da_heads, da_w)
    cache_v2 = cache_v.reshape(depth, n_pool, page * da_heads, da_w)
    outs = [[] for _ in range(6)]
    for l in range(depth):
        lam_init = 0.8 - 0.6 * math.exp(-0.3 * l)
        lamv = jnp.stack([da_lambda_q1[l], da_lambda_k1[l], da_lambda_q2[l], da_lambda_k2[l]]).astype(F32)
        ada = _ada_call(jnp.concatenate([c_prompt, c_sample]).astype(F32), w_ada[l], b_ada[l])
        ada_p = jnp.transpose(ada[:bp].reshape(bp, 9, d), (1, 0, 2)).reshape(9, bp, 1, d)
        ada_s = jnp.transpose(jnp.repeat(ada[bp:], t_new, axis=0).reshape(n_s, 9, d), (1, 0, 2)).reshape(9, 1, n_s, d)
        wf_in = ffn_w_in[l].astype(BF16)
        wf_out = ffn_w_out[l].astype(BF16)
        w_in_l = w_in[l].astype(BF16)
        wpa, wpb, wo = w_pa[l].astype(BF16), w_pb[l].astype(BF16), w_o[l].astype(BF16)

        def trunk(y, ada4, tm, tps, attend, hgrn):
            x1 = _ffn_ln_call(y, ada4, 0, wf_in[0], wf_out[0], ln_g[l, 0], ln_b[l, 0],
                              alpha=alpha, tm=tm, tiles_per_seq=tps)
            qa, lf, ka, va, ga, q16, kb, k16, vb, v16, sga, sgb = _inproj_call(
                x1, ada4, w_in_l, hg_lower_bound.astype(F32), layer=l, hg_key=hg_key,
                da_head_dim=da_w // 2, tm=tm, tiles_per_seq=tps)
            oa, s_new = hgrn(qa, lf, ka, va, ga)
            ob = attend(q16, k16, v16)
            x2 = _merge_call(x1, ada4, oa, ob, sga, sgb, wpa, wpb, wo, ln_g[l, 1], ln_b[l, 1],
                             alpha=alpha, tm=tm, tiles_per_seq=tps)
            y_new = _ffn_ln_call(x2, ada4, 2, wf_in[1], wf_out[1], ln_g[l, 2], ln_b[l, 2],
                                 alpha=alpha, tm=tm, tiles_per_seq=tps)
            return y_new, kb, vb, s_new

        def hgrn_p(qa, lf, ka, va, ga):
            r3 = lambda a: a.reshape(bp, seq, d)
            oa, s_new = _hgrn_call(r3(qa), r3(lf), r3(ka), r3(va), r3(ga), hg_norm_g[l], None, chunk=hg_chunk)
            return oa.reshape(n_p, d), s_new

        def attend_p(q16, k16, v16):
            r3 = lambda a: a.reshape(bp, seq, d)
            ob = _attn_prompt_call(r3(q16), r3(k16), r3(v16), lamv, bias_p, da_subln_g[l],
                                   heads=da_heads, t=t_attn, lam_init=lam_init)
            return ob.reshape(n_p, d)

        def hgrn_s(qa, lf, ka, va, ga):
            pad = lambda a: jnp.pad(a.reshape(bs, t_new, d), ((0, 0), (0, t_pad - t_new), (0, 0)))
            oa, s_new = _hgrn_call(pad(qa), pad(lf), pad(ka), pad(va), pad(ga), hg_norm_g[l],
                                   state_hgrn[l].astype(F32), chunk=t_pad)
            return oa[:, :t_new].reshape(n_s, d), s_new

        def attend_s(q16, k16, v16):
            qt = jnp.transpose(q16.reshape(bs, t_new, da_heads, da_w), (0, 2, 1, 3))
            qh = jnp.where(own_map, jnp.concatenate([qt, qt], axis=2), jnp.zeros((), BF16))
            qh = jnp.pad(qh, ((0, 0), (0, 0), (0, BF16_ROWS - 2 * t_new), (0, 0)))
            qh = qh.reshape(bs, da_heads // 2, 2, BF16_ROWS, da_w)
            zq = jnp.zeros_like(qh[:, :, 0])
            qh = jnp.concatenate([jnp.concatenate([qh[:, :, 0], zq], axis=-1),
                                  jnp.concatenate([zq, qh[:, :, 1]], axis=-1)], axis=2)
            padn = lambda a: jnp.pad(a.reshape(bs, t_new, d), ((0, 0), (0, BF16_ROWS - t_new), (0, 0)))
            ob = _attn_sample_call(page_table.astype(jnp.int32), lamv, qh, cache_k2, cache_v2, l,
                                   padn(k16), padn(v16), sb_far, sb_last, sb_new, da_subln_g[l],
                                   group=group, t_new=t_new, lam_init=lam_init)
            return ob[:, :t_new].reshape(n_s, d).astype(BF16)

        y_p, k_p, v_p, s_p = trunk(y_p, ada_p, tm_p, tps_p, attend_p, hgrn_p)
        y_s, k_s, v_s, s_s = trunk(y_s, ada_s, tm_s, 1, attend_s, hgrn_s)
        for lst, val in zip(outs, (k_p.reshape(bp, seq, da_heads, da_w), v_p.reshape(bp, seq, da_heads, da_w), s_p,
                                   k_s.reshape(bs, t_new, da_heads, da_w), v_s.reshape(bs, t_new, da_heads, da_w), s_s)):
            lst.append(val)

    k_prompt, v_prompt, st_prompt, k_sample, v_sample, st_sample = (jnp.stack(o) for o in outs)
    return (y_p.reshape(bp, seq, d), y_s.reshape(bs, t_new, d), k_prompt, v_prompt, st_prompt,
            k_sample, v_sample, st_sample)
```

```python
import functools
import math

import jax
import jax.numpy as jnp
import numpy as np
from jax import lax
from jax.experimental import pallas as pl
from jax.experimental.pallas import tpu as pltpu

F32 = jnp.float32
BF16 = jnp.bfloat16

LN_EPS = 1e-5
RMS_EPS = 1e-5
REL_BUCKETS = 32
REL_MAX_DIST = 128
MASKED = -1e30
LOG2E = math.log2(math.e)
VMEM_LIMIT_BYTES = 56 * 1024 * 1024
LANES = 128
BF16_ROWS = 16


def _cparams(sem):
    return pltpu.CompilerParams(dimension_semantics=sem, vmem_limit_bytes=VMEM_LIMIT_BYTES)


def _sigmoid(x):
    return 1.0 / (1.0 + jnp.exp(-x))


def _dot(a, b):
    return jnp.dot(a, b, preferred_element_type=F32)


def _dot_nt(a, b):
    return lax.dot_general(a, b, (((1,), (1,)), ((), ())), preferred_element_type=F32)


def _dot_tn(a, b):
    return lax.dot_general(a, b, (((0,), (0,)), ((), ())), preferred_element_type=F32)


def _layer_norm(x, g, b):
    mu = jnp.mean(x, axis=-1, keepdims=True)
    xc = x - mu
    var = jnp.mean(xc * xc, axis=-1, keepdims=True)
    return xc * lax.rsqrt(var + LN_EPS) * g + b


def _ada_kernel(c_ref, w_ref, b_ref, o_ref):
    c = c_ref[...]
    s = c * _sigmoid(c)
    o_ref[...] = jnp.dot(s, w_ref[...], preferred_element_type=F32,
                         precision=lax.Precision.HIGHEST) + b_ref[...]


def _ada_call(c, w, b):
    rows, d = c.shape
    width = w.shape[1]
    tn = 1152 if width % 1152 == 0 else width
    return pl.pallas_call(
        _ada_kernel,
        grid=(width // tn,),
        in_specs=[pl.BlockSpec((rows, d), lambda j: (0, 0)),
                  pl.BlockSpec((d, tn), lambda j: (0, j)),
                  pl.BlockSpec((1, tn), lambda j: (0, j))],
        out_specs=pl.BlockSpec((rows, tn), lambda j: (0, j)),
        out_shape=jax.ShapeDtypeStruct((rows, width), F32),
        compiler_params=_cparams(("arbitrary",)),
        name="ada_proj",
    )(c, w, b.reshape(1, width))


def _ada_spec(ada4, k, tm, tiles_per_seq):
    d = ada4.shape[-1]
    if ada4.shape[2] == 1:
        return pl.BlockSpec((None, None, 1, d), lambda i, *_: (k, i // tiles_per_seq, 0, 0))
    return pl.BlockSpec((None, None, tm, d), lambda i, *_: (k, 0, i, 0))


def _ffn_ln_kernel(x_ref, sh_ref, sc_ref, gt_ref, wg_ref, wu_ref, wo_ref, lg_ref, lb_ref, o_ref, *, alpha):
    x = x_ref[...]
    h = (x * (1.0 + sc_ref[...]) + sh_ref[...]).astype(BF16)
    gate = _dot(h, wg_ref[...])
    up = _dot(h, wu_ref[...])
    a = (gate * _sigmoid(gate) * up).astype(BF16)
    y = alpha * x + 0.5 * gt_ref[...] * _dot(a, wo_ref[...])
    o_ref[...] = _layer_norm(y, lg_ref[...], lb_ref[...])


def _resident(shape, index_map):
    return pl.BlockSpec(shape, index_map, pipeline_mode=pl.Buffered(1))


def _ffn_ln_call(x, ada4, sub, w_in, w_out, ln_g, ln_b, *, alpha, tm, tiles_per_seq):
    n, d = x.shape
    f = w_out.shape[0]
    return pl.pallas_call(
        functools.partial(_ffn_ln_kernel, alpha=alpha),
        grid=(n // tm,),
        in_specs=[pl.BlockSpec((tm, d), lambda i: (i, 0)),
                  _ada_spec(ada4, 3 * sub + 0, tm, tiles_per_seq),
                  _ada_spec(ada4, 3 * sub + 1, tm, tiles_per_seq),
                  _ada_spec(ada4, 3 * sub + 2, tm, tiles_per_seq),
                  _resident((d, f), lambda i: (0, 0)),
                  _resident((d, f), lambda i: (0, 1)),
                  _resident((f, d), lambda i: (0, 0)),
                  pl.BlockSpec((1, d), lambda i: (0, 0)),
                  pl.BlockSpec((1, d), lambda i: (0, 0))],
        out_specs=pl.BlockSpec((tm, d), lambda i: (i, 0)),
        out_shape=jax.ShapeDtypeStruct((n, d), F32),
        compiler_params=_cparams(("parallel",)),
        name="ffn_ln",
    )(x, ada4, ada4, ada4, w_in, w_in, w_out, ln_g.reshape(1, d), ln_b.reshape(1, d))


def _inproj_kernel(x_ref, sh_ref, sc_ref, w_ref, lbp_ref,
                   qa_ref, lf_ref, ka_ref, va_ref, ga_ref, q16_ref, kb_ref, k16_ref, vb_ref, v16_ref,
                   sga_ref, sgb_ref, *, layer, hg_scale, da_scale):
    h = (x_ref[...] * (1.0 + sc_ref[...]) + sh_ref[...]).astype(BF16)
    d = h.shape[1]

    def z(j):
        return _dot(h, w_ref[:, j * d:(j + 1) * d])

    z0 = z(0)
    qa_ref[...] = (z0 * _sigmoid(z0) * hg_scale).astype(BF16)
    p = lbp_ref[...]
    e = jnp.exp(p - jnp.max(p, axis=0, keepdims=True))
    lb = jnp.sum(e[:layer + 1], axis=0, keepdims=True) / jnp.sum(e, axis=0, keepdims=True)
    f = lb + (1.0 - lb) * _sigmoid(z(1))
    lf_ref[...] = jnp.log(f)
    ka_ref[...] = (1.0 - f).astype(BF16)
    va_ref[...] = z(2).astype(BF16)
    z3 = z(3)
    ga_ref[...] = (z3 * _sigmoid(z3)).astype(BF16)
    q16_ref[...] = (z(4) * da_scale).astype(BF16)
    z5 = z(5)
    kb_ref[...] = z5
    k16_ref[...] = z5.astype(BF16)
    z6 = z(6)
    vb_ref[...] = z6
    v16_ref[...] = z6.astype(BF16)
    sga_ref[...] = _sigmoid(z(7)).astype(BF16)
    sgb_ref[...] = _sigmoid(z(8)).astype(BF16)


def _inproj_call(x, ada4, w_in, lb_params, *, layer, hg_key, da_head_dim, tm, tiles_per_seq):
    n, d = x.shape
    assert w_in.shape[1] == 9 * d
    kern = functools.partial(_inproj_kernel, layer=layer, hg_scale=hg_key ** -0.5,
                             da_scale=da_head_dim ** -0.5 * LOG2E)
    out_dtypes = [BF16, F32, BF16, BF16, BF16, BF16, F32, BF16, F32, BF16, BF16, BF16]
    row_spec = pl.BlockSpec((tm, d), lambda i: (i, 0))
    return pl.pallas_call(
        kern,
        grid=(n // tm,),
        in_specs=[row_spec,
                  _ada_spec(ada4, 3, tm, tiles_per_seq),
                  _ada_spec(ada4, 4, tm, tiles_per_seq),
                  _resident(w_in.shape, lambda i: (0, 0)),
                  pl.BlockSpec(lb_params.shape, lambda i: (0, 0))],
        out_specs=[row_spec] * len(out_dtypes),
        out_shape=[jax.ShapeDtypeStruct((n, d), dt) for dt in out_dtypes],
        compiler_params=_cparams(("parallel",)),
        name="in_proj",
    )(x, ada4, ada4, w_in, lb_params)


def _hgrn_kernel(*refs, chunk, heads, has_s0):
    if has_s0:
        q_ref, g_ref, k_ref, v_ref, gt_ref, ng_ref, s0_ref, o_ref, s_ref = refs
    else:
        q_ref, g_ref, k_ref, v_ref, gt_ref, ng_ref, o_ref, s_ref = refs
        s0_ref = None
    C = chunk

    @pl.when(pl.program_id(1) == 0)
    def _():
        if has_s0:
            s_ref[...] = s0_ref[...]
        else:
            s_ref[...] = jnp.zeros(s_ref.shape, F32)

    row = lax.broadcasted_iota(jnp.int32, (C, LANES), 0)
    ti = lax.broadcasted_iota(jnp.int32, (C, C), 0)
    si = lax.broadcasted_iota(jnp.int32, (C, C), 1)
    shifts, halves = [], []
    hh = 1
    while hh < C:
        shifts.append((hh, row >= hh))
        right = (row & (2 * hh - 1)) >= hh
        pair = (((ti ^ si) & ~(2 * hh - 1)) == 0) & ((ti & (2 * hh - 1)) >= hh) & ((si & (2 * hh - 1)) < hh)
        halves.append((hh, right, pair))
        hh *= 2
    eye = lax.broadcasted_iota(jnp.int32, (LANES, LANES), 0) == lax.broadcasted_iota(jnp.int32, (LANES, LANES), 1)
    sign = jnp.uint32(0x80000000)
    ng = ng_ref[...]

    for h in range(heads):
        sl = slice(h * LANES, (h + 1) * LANES)
        b = g_ref[0, :, sl] * LOG2E
        for sh, keep in shifts:
            b = b + jnp.where(keep, pltpu.roll(b, sh, 0), 0.0)
        q = q_ref[0, :, sl].astype(F32)
        k = k_ref[0, :, sl].astype(F32)
        vh = v_ref[0, :, sl]
        a = jnp.zeros((C, C), F32)
        u = b
        for hh, right, pair in halves:
            d = b - jnp.where(right, pltpu.roll(u, hh, 0), u)
            e = jnp.exp2(lax.bitcast_convert_type(lax.bitcast_convert_type(d, jnp.uint32) | sign, F32))
            a = a + jnp.where(pair, _dot_nt((q * e).astype(BF16), (k * e).astype(BF16)), 0.0)
            u = jnp.where(right, u, pltpu.roll(u, C - hh, 0))
        b_last = u
        s_old = s_ref[0, h]
        diag = jnp.sum(q * k, axis=1, keepdims=True)
        o = (_dot((q * jnp.exp2(b)).astype(BF16), s_old.astype(BF16)) + _dot(a.astype(BF16), vh)
             + diag * vh.astype(F32))
        ms = jnp.mean(o * o, axis=1, keepdims=True)
        o_ref[0, :, sl] = (o * lax.rsqrt(ms + RMS_EPS) * ng * gt_ref[0, :, sl].astype(F32)).astype(BF16)
        bl_col = jnp.sum(jnp.where(eye, jnp.broadcast_to(b_last[0:1], (LANES, LANES)), 0.0),
                         axis=1, keepdims=True)
        k_st = (k * jnp.exp2(b_last - b)).astype(BF16)
        s_ref[0, h] = jnp.exp2(bl_col) * s_old + _dot_tn(k_st, vh)


def _hgrn_call(qa, lf, ka, va, ga, norm_g, s0, *, chunk):
    bsz, t, w = qa.shape
    heads = w // LANES
    has_s0 = s0 is not None
    kern = functools.partial(_hgrn_kernel, chunk=chunk, heads=heads, has_s0=has_s0)
    tok = pl.BlockSpec((1, chunk, w), lambda b, c: (b, c, 0))
    st = pl.BlockSpec((1, heads, LANES, LANES), lambda b, c: (b, 0, 0, 0))
    in_specs = [tok, tok, tok, tok, tok, pl.BlockSpec((1, LANES), lambda b, c: (0, 0))]
    args = [qa, lf, ka, va, ga, norm_g.reshape(1, LANES)]
    if has_s0:
        in_specs.append(st)
        args.append(s0)
    return pl.pallas_call(
        kern,
        grid=(bsz, t // chunk),
        in_specs=in_specs,
        out_specs=[tok, st],
        out_shape=[jax.ShapeDtypeStruct((bsz, t, w), BF16),
                   jax.ShapeDtypeStruct((bsz, heads, LANES, LANES), F32)],
        compiler_params=_cparams(("parallel", "arbitrary")),
        name="hgrn2",
    )(*args)


def _relbias_kernel(tab_ref, o_ref, *, base, stride):
    hm = pl.program_id(0)
    rows, cols = o_ref.shape
    off = base + pl.program_id(1) * stride
    n = off + lax.broadcasted_iota(jnp.int32, (rows, cols), 0) - lax.broadcasted_iota(jnp.int32, (rows, cols), 1)
    nn = jnp.maximum(n, 0)
    max_exact = REL_BUCKETS // 2
    nf = jnp.maximum(nn, 1).astype(F32)
    large = max_exact + (jnp.log(nf / max_exact) / math.log(REL_MAX_DIST / max_exact)
                         * (REL_BUCKETS - max_exact)).astype(jnp.int32)
    bucket = jnp.where(nn < max_exact, nn, jnp.minimum(large, REL_BUCKETS - 1))
    val = jnp.zeros((rows, cols), F32)
    for bk in range(REL_BUCKETS):
        val = jnp.where(bucket == bk, tab_ref[hm * REL_BUCKETS + bk], val)
    o_ref[...] = jnp.where(n < 0, MASKED, val * LOG2E)


def _relbias_call(tab_flat, heads, kinds, rows, cols, base, stride):
    return pl.pallas_call(
        functools.partial(_relbias_kernel, base=base, stride=stride),
        grid=(heads * 2, kinds),
        in_specs=[pl.BlockSpec(memory_space=pltpu.SMEM)],
        out_specs=pl.BlockSpec((None, None, None, rows, cols), lambda i, kd: (i // 2, kd, i % 2, 0, 0)),
        out_shape=jax.ShapeDtypeStruct((heads, kinds, 2, rows, cols), F32),
        compiler_params=_cparams(("arbitrary", "arbitrary")),
        name="rel_bias",
    )(tab_flat)


def _lambda(lamv_ref, lam_init):
    lv = lamv_ref[...]
    s1 = jnp.sum(lv[0:1] * lv[1:2], axis=1, keepdims=True)
    s2 = jnp.sum(lv[2:3] * lv[3:4], axis=1, keepdims=True)
    return jnp.exp(s1) - jnp.exp(s2) + lam_init


def _attn_prompt_kernel(lamv_ref, q_ref, k_ref, v_ref, bias_ref, g_ref, o_ref,
                        q2_scr, vx_scr, s_scr, mx_scr, acc_scr, *, t, half, lam_init):
    i = pl.program_id(2)

    @pl.when(i == 0)
    def _():
        vx_scr[:, 0:LANES] = v_ref[0]
        vx_scr[:, LANES:2 * LANES] = jnp.ones((vx_scr.shape[0], LANES), BF16)

    q = q_ref[0]
    lane = lax.broadcasted_iota(jnp.int32, q.shape, 1)
    zero = jnp.zeros(q.shape, q.dtype)
    q2_scr[0:t] = jnp.where(lane < half, q, zero)
    q2_scr[t:2 * t] = jnp.where(lane >= half, q, zero)
    mx_scr[...] = jnp.full(mx_scr.shape, MASKED, F32)
    n_sub = t // LANES

    def logits(j, carry):
        start = pl.multiple_of(j * t, t)
        s = _dot_nt(q2_scr[...], k_ref[0, pl.ds(start, t), :]) + bias_ref[0, jnp.minimum(i - j, 2)]
        s_scr[j] = s
        mx = mx_scr[...]
        for u in range(n_sub):
            mx = jnp.maximum(mx, s[:, u * LANES:(u + 1) * LANES])
        mx_scr[...] = mx
        return carry

    lax.fori_loop(0, i + 1, logits, 0)

    m = jnp.max(mx_scr[...], axis=1, keepdims=True)
    mx_scr[...] = jnp.broadcast_to(m, mx_scr.shape)
    acc_scr[...] = jnp.zeros(acc_scr.shape, F32)

    def weighted(j, carry):
        start = pl.multiple_of(j * t, t)
        s = s_scr[j]
        mb = mx_scr[...]
        p = jnp.concatenate([jnp.exp2(s[:, u * LANES:(u + 1) * LANES] - mb).astype(BF16)
                             for u in range(n_sub)], axis=1)
        acc_scr[...] += _dot(p, vx_scr[pl.ds(start, t), :])
        return carry

    lax.fori_loop(0, i + 1, weighted, 0)

    acc = acc_scr[...]
    on = acc[:, 0:LANES] / acc[:, LANES:2 * LANES]
    o = on[0:t] - _lambda(lamv_ref, lam_init) * on[t:2 * t]
    ms = jnp.mean(o * o, axis=1, keepdims=True)
    o_ref[0] = (o * lax.rsqrt(ms + RMS_EPS) * g_ref[...] * (1.0 - lam_init)).astype(BF16)


def _attn_prompt_call(q16, k16, v16, lamv, bias, subln_g, *, heads, t, lam_init):
    bsz, seq, w = q16.shape
    e = w // heads
    nq = seq // t
    kern = functools.partial(_attn_prompt_kernel, t=t, half=e // 2, lam_init=lam_init)
    return pl.pallas_call(
        kern,
        grid=(heads, bsz, nq),
        in_specs=[pl.BlockSpec(lamv.shape, lambda h, b, i: (0, 0)),
                  pl.BlockSpec((1, t, e), lambda h, b, i: (b, i, h)),
                  pl.BlockSpec((1, seq, e), lambda h, b, i: (b, 0, h)),
                  pl.BlockSpec((1, seq, e), lambda h, b, i: (b, 0, h)),
                  pl.BlockSpec((1, 3, 2 * t, t), lambda h, b, i: (h, 0, 0, 0)),
                  pl.BlockSpec((1, e), lambda h, b, i: (0, 0))],
        out_specs=pl.BlockSpec((1, t, e), lambda h, b, i: (b, i, h)),
        out_shape=jax.ShapeDtypeStruct((bsz, seq, w), BF16),
        scratch_shapes=[pltpu.VMEM((2 * t, e), BF16), pltpu.VMEM((seq, 2 * e), BF16),
                        pltpu.VMEM((nq, 2 * t, t), F32), pltpu.VMEM((2 * t, LANES), F32),
                        pltpu.VMEM((2 * t, 2 * e), F32)],
        compiler_params=_cparams(("arbitrary", "arbitrary", "arbitrary")),
        name="attn_prompt",
    )(lamv, q16, k16, v16, bias, subln_g.reshape(1, e))


def _attn_sample_kernel(pt_ref, lamv_ref, qh_ref, *rest, group, n_steps, heads, t_new, lam_init):
    k_refs, v_refs = rest[:group], rest[group:2 * group]
    kn_ref, vn_ref, bfar_ref, blast_ref, bnew_ref, g_ref, o_ref, m_scr, l_scr, acc_scr = rest[2 * group:]
    p = pl.program_id(1)
    page = k_refs[0].shape[0] // heads
    rows = qh_ref.shape[2] // 2

    @pl.when(p == 0)
    def _():
        m_scr[...] = jnp.full(m_scr.shape, MASKED, F32)
        l_scr[...] = jnp.zeros(l_scr.shape, F32)
        acc_scr[...] = jnp.zeros(acc_scr.shape, F32)

    def update(c, s, vals):
        m_old = m_scr[c]
        m_new = jnp.maximum(m_old, jnp.max(s, axis=1, keepdims=True))
        alpha = jnp.exp2(m_old - m_new)
        pr = jnp.exp2(s - m_new)
        l_scr[c] = alpha * l_scr[c] + jnp.sum(pr, axis=1, keepdims=True)
        acc_scr[c] = alpha * acc_scr[c] + _dot(pr.astype(BF16), vals)
        m_scr[c] = m_new

    def pair_pages(refs, c):
        even, odd = pl.ds(2 * c, page, stride=heads), pl.ds(2 * c + 1, page, stride=heads)
        return jnp.concatenate([jnp.concatenate([r[even, :], r[odd, :]], axis=1) for r in refs],
                               axis=0).astype(BF16)

    last = p == n_steps - 1
    for c in range(heads // 2):
        s = _dot_nt(qh_ref[0, c], pair_pages(k_refs, c)) + jnp.where(last, blast_ref[c], bfar_ref[c])
        update(c, s, pair_pages(v_refs, c))

    @pl.when(last)
    def _():
        lam = _lambda(lamv_ref, lam_init)
        for c in range(heads // 2):
            sl2 = slice(2 * c * LANES, (2 * c + 2) * LANES)
            update(c, _dot_nt(qh_ref[0, c], kn_ref[0, :, sl2]) + bnew_ref[c], vn_ref[0, :, sl2])
            on2 = acc_scr[c] / l_scr[c]
            for side in range(2):
                on = on2[side * rows:(side + 1) * rows, side * LANES:(side + 1) * LANES]
                o = on - lam * pltpu.roll(on, rows - t_new, 0)
                ms = jnp.mean(o * o, axis=1, keepdims=True)
                sl = slice((2 * c + side) * LANES, (2 * c + side + 1) * LANES)
                o_ref[0, :, sl] = o * lax.rsqrt(ms + RMS_EPS) * g_ref[...] * (1.0 - lam_init)


def _attn_sample_call(page_table, lamv, qh, cache_k, cache_v, layer, k_new, v_new, b_far, b_last, b_new, subln_g,
                      *, group, t_new, lam_init):
    bsz, pairs, rows2, e2 = qh.shape
    heads, rows, e = 2 * pairs, rows2 // 2, e2 // 2
    n_pages = page_table.shape[1]
    n_steps = n_pages // group
    page_rows = cache_k.shape[2]
    w = heads * e
    rows_new = k_new.shape[1]
    kern = functools.partial(_attn_sample_kernel, group=group, n_steps=n_steps, heads=heads, t_new=t_new,
                             lam_init=lam_init)

    def page_spec(r):
        return pl.BlockSpec((None, None, page_rows, e),
                            lambda b, p, pt: (layer, pt[b * n_pages + p * group + r], 0, 0))

    fixed = lambda shape: pl.BlockSpec(shape, lambda b, p, pt: (0,) * len(shape))
    grid_spec = pltpu.PrefetchScalarGridSpec(
        num_scalar_prefetch=1,
        grid=(bsz, n_steps),
        in_specs=[fixed(lamv.shape),
                  pl.BlockSpec((1, pairs, rows2, e2), lambda b, p, pt: (b, 0, 0, 0))]
                 + [page_spec(r) for r in range(group)] * 2
                 + [pl.BlockSpec((1, rows_new, w), lambda b, p, pt: (b, 0, 0)),
                    pl.BlockSpec((1, rows_new, w), lambda b, p, pt: (b, 0, 0)),
                    fixed(b_far.shape), fixed(b_last.shape), fixed(b_new.shape), fixed((1, e))],
        out_specs=pl.BlockSpec((1, rows, w), lambda b, p, pt: (b, 0, 0)),
        scratch_shapes=[pltpu.VMEM((pairs, rows2, 1), F32), pltpu.VMEM((pairs, rows2, 1), F32),
                        pltpu.VMEM((pairs, rows2, e2), F32)],
    )
    return pl.pallas_call(
        kern,
        grid_spec=grid_spec,
        out_shape=jax.ShapeDtypeStruct((bsz, rows, w), F32),
        compiler_params=_cparams(("parallel", "arbitrary")),
        name="attn_sample",
    )(page_table.reshape(-1), lamv, qh, *([cache_k] * group), *([cache_v] * group), k_new, v_new,
      b_far, b_last, b_new, subln_g.reshape(1, e))


def _merge_kernel(x_ref, gt_ref, oa_ref, ob_ref, sga_ref, sgb_ref, wpa_ref, wpb_ref, wo_ref, lg_ref, lb_ref,
                  o_ref, *, alpha):
    m = (sga_ref[...].astype(F32) * _dot(oa_ref[...], wpa_ref[...])
         + sgb_ref[...].astype(F32) * _dot(ob_ref[...], wpb_ref[...]))
    r = _dot(m.astype(BF16), wo_ref[...])
    o_ref[...] = _layer_norm(alpha * x_ref[...] + gt_ref[...] * r, lg_ref[...], lb_ref[...])


def _merge_call(x, ada4, oa, ob, sga, sgb, w_pa, w_pb, w_o, ln_g, ln_b, *, alpha, tm, tiles_per_seq):
    n, d = x.shape
    row = pl.BlockSpec((tm, d), lambda i: (i, 0))
    wsp = pl.BlockSpec((d, d), lambda i: (0, 0))
    vec = pl.BlockSpec((1, d), lambda i: (0, 0))
    return pl.pallas_call(
        functools.partial(_merge_kernel, alpha=alpha),
        grid=(n // tm,),
        in_specs=[row, _ada_spec(ada4, 5, tm, tiles_per_seq), row, row, row, row, wsp, wsp, wsp, vec, vec],
        out_specs=row,
        out_shape=jax.ShapeDtypeStruct((n, d), F32),
        compiler_params=_cparams(("parallel",)),
        name="merge",
    )(x, ada4, oa, ob, sga, sgb, w_pa, w_pb, w_o, ln_g.reshape(1, d), ln_b.reshape(1, d))


def _row_tile(n, pref):
    tm = min(n, pref)
    while n % tm:
        tm //= 2
    return tm


def kernel(x_prompt, x_sample, cache_k, cache_v, state_hgrn, page_table, c_prompt, c_sample, w_ada, b_ada, ln_g, ln_b, ffn_w_in, ffn_w_out, w_in, hg_lower_bound, hg_norm_g, da_lambda_q1, da_lambda_k1, da_lambda_q2, da_lambda_k2, da_subln_g, rel_bias, w_pa, w_pb, w_o):
    depth = w_in.shape[0]
    bp, seq, d = x_prompt.shape
    bs, t_new, _ = x_sample.shape
    n_pool, page, da_heads, da_w = cache_k.shape[1:]
    hg_heads, hg_key = state_hgrn.shape[2], state_hgrn.shape[3]
    n_pages = page_table.shape[1]
    alpha = (2.0 * depth) ** 0.25
    t_attn = min(seq, 512)
    assert t_attn >= REL_MAX_DIST - 1 and page >= REL_MAX_DIST - 1
    assert da_w == LANES and hg_key == LANES and d == hg_heads * LANES == da_heads * LANES
    assert 2 * t_new <= BF16_ROWS

    group = 8
    while n_pages % group:
        group //= 2
    tab = jnp.transpose(rel_bias.astype(F32), (2, 1, 0)).reshape(-1)
    bias_p = _relbias_call(tab, da_heads, 3, t_attn, t_attn, 0, t_attn).reshape(da_heads, 3, 2 * t_attn, t_attn)
    far_tile = _relbias_call(tab, da_heads, 1, 8, LANES, 2 * REL_MAX_DIST + LANES, 0)
    last_tile = _relbias_call(tab, da_heads, 1, 8, LANES, page, 0)
    new_tile = _relbias_call(tab, da_heads, 1, 8, LANES, 0, 0)

    def sample_rows(tile, cols):
        a = tile[:, 0, :, :t_new, :cols].reshape(da_heads, 2 * t_new, cols)
        return jnp.pad(a, ((0, 0), (0, BF16_ROWS - 2 * t_new), (0, 0)))

    def pair_rows(a):
        return a.reshape(da_heads // 2, 2 * BF16_ROWS, a.shape[-1])

    sb_far = sample_rows(far_tile, 1)
    sb_last = jnp.concatenate([jnp.broadcast_to(sb_far, (da_heads, BF16_ROWS, (group - 1) * page)),
                               sample_rows(last_tile, page)], axis=2)
    sb_new = pair_rows(sample_rows(new_tile, BF16_ROWS))
    sb_far, sb_last = pair_rows(sb_far), pair_rows(sb_last)

    n_p, n_s = bp * seq, bs * t_new
    tm_p = _row_tile(seq, 512)
    tps_p = seq // tm_p
    tm_s = _row_tile(n_s, 512)
    hg_chunk = min(seq, 64)
    t_pad = BF16_ROWS
    map_of_lane = (jnp.arange(LANES) // (LANES // 2))[None, :]
    map_of_row = (jnp.arange(2 * t_new) // t_new)[:, None]
    own_map = (map_of_lane == map_of_row)[None, None]

    y_p = x_prompt.reshape(n_p, d)
    y_s = x_sample.reshape(n_s, d)
    cache_k2 = cache_k.reshape(depth, n_pool, page * da_heads, da_w)
    cache_v2 = cache_v.reshape(depth, n_pool, page * da_heads, da_w)
    outs = [[] for _ in range(6)]
    for l in range(depth):
        lam_init = 0.8 - 0.6 * math.exp(-0.3 * l)
        lamv = jnp.stack([da_lambda_q1[l], da_lambda_k1[l], da_lambda_q2[l], da_lambda_k2[l]]).astype(F32)
        ada = _ada_call(jnp.concatenate([c_prompt, c_sample]).astype(F32), w_ada[l], b_ada[l])
        ada_p = jnp.transpose(ada[:bp].reshape(bp, 9, d), (1, 0, 2)).reshape(9, bp, 1, d)
        ada_s = jnp.transpose(jnp.repeat(ada[bp:], t_new, axis=0).reshape(n_s, 9, d), (1, 0, 2)).reshape(9, 1, n_s, d)
        wf_in = ffn_w_in[l].astype(BF16)
        wf_out = ffn_w_out[l].astype(BF16)
        w_in_l = w_in[l].astype(BF16)
        wpa, wpb, wo = w_pa[l].astype(BF16), w_pb[l].astype(BF16), w_o[l].astype(BF16)

        def trunk(y, ada4, tm, tps, attend, hgrn):
            x1 = _ffn_ln_call(y, ada4, 0, wf_in[0], wf_out[0], ln_g[l, 0], ln_b[l, 0],
                              alpha=alpha, tm=tm, tiles_per_seq=tps)
            qa, lf, ka, va, ga, q16, kb, k16, vb, v16, sga, sgb = _inproj_call(
                x1, ada4, w_in_l, hg_lower_bound.astype(F32), layer=l, hg_key=hg_key,
                da_head_dim=da_w // 2, tm=min(tm, 256), tiles_per_seq=tps * (tm // min(tm, 256)))
            oa, s_new = hgrn(qa, lf, ka, va, ga)
            ob = attend(q16, k16, v16)
            x2 = _merge_call(x1, ada4, oa, ob, sga, sgb, wpa, wpb, wo, ln_g[l, 1], ln_b[l, 1],
                             alpha=alpha, tm=tm, tiles_per_seq=tps)
            y_new = _ffn_ln_call(x2, ada4, 2, wf_in[1], wf_out[1], ln_g[l, 2], ln_b[l, 2],
                                 alpha=alpha, tm=tm, tiles_per_seq=tps)
            return y_new, kb, vb, s_new

        def hgrn_p(qa, lf, ka, va, ga):
            r3 = lambda a: a.reshape(bp, seq, d)
            oa, s_new = _hgrn_call(r3(qa), r3(lf), r3(ka), r3(va), r3(ga), hg_norm_g[l], None, chunk=hg_chunk)
            return oa.reshape(n_p, d), s_new

        def attend_p(q16, k16, v16):
            r3 = lambda a: a.reshape(bp, seq, d)
            ob = _attn_prompt_call(r3(q16), r3(k16), r3(v16), lamv, bias_p, da_subln_g[l],
                                   heads=da_heads, t=t_attn, lam_init=lam_init)
            return ob.reshape(n_p, d)

        def hgrn_s(qa, lf, ka, va, ga):
            pad = lambda a: jnp.pad(a.reshape(bs, t_new, d), ((0, 0), (0, t_pad - t_new), (0, 0)))
            oa, s_new = _hgrn_call(pad(qa), pad(lf), pad(ka), pad(va), pad(ga), hg_norm_g[l],
                                   state_hgrn[l].astype(F32), chunk=t_pad)
            return oa[:, :t_new].reshape(n_s, d), s_new

        def attend_s(q16, k16, v16):
            qt = jnp.transpose(q16.reshape(bs, t_new, da_heads, da_w), (0, 2, 1, 3))
            qh = jnp.where(own_map, jnp.concatenate([qt, qt], axis=2), jnp.zeros((), BF16))
            qh = jnp.pad(qh, ((0, 0), (0, 0), (0, BF16_ROWS - 2 * t_new), (0, 0)))
            qh = qh.reshape(bs, da_heads // 2, 2, BF16_ROWS, da_w)
            zq = jnp.zeros_like(qh[:, :, 0])
            qh = jnp.concatenate([jnp.concatenate([qh[:, :, 0], zq], axis=-1),
                                  jnp.concatenate([zq, qh[:, :, 1]], axis=-1)], axis=2)
            padn = lambda a: jnp.pad(a.reshape(bs, t_new, d), ((0, 0), (0, BF16_ROWS - t_new), (0, 0)))
            ob = _attn_sample_call(page_table.astype(jnp.int32), lamv, qh, cache_k2, cache_v2, l,
                                   padn(k16), padn(v16), sb_far, sb_last, sb_new, da_subln_g[l],
                                   group=group, t_new=t_new, lam_init=lam_init)
            return ob[:, :t_new].reshape(n_s, d).astype(BF16)

        y_p, k_p, v_p, s_p = trunk(y_p, ada_p, tm_p, tps_p, attend_p, hgrn_p)
        y_s, k_s, v_s, s_s = trunk(y_s, ada_s, tm_s, 1, attend_s, hgrn_s)
        for lst, val in zip(outs, (k_p.reshape(bp, seq, da_heads, da_w), v_p.reshape(bp, seq, da_heads, da_w), s_p,
                                   k_s.reshape(bs, t_new, da_heads, da_w), v_s.reshape(bs, t_new, da_heads, da_w), s_s)):
            lst.append(val)

    k_prompt, v_prompt, st_prompt, k_sample, v_sample, st_sample = (jnp.stack(o) for o in outs)
    return (y_p.reshape(bp, seq, d), y_s.reshape(bs, t_new, d), k_prompt, v_prompt, st_prompt,
            k_sample, v_sample, st_sample)
```

```python
import functools
import math

import jax
import jax.numpy as jnp
import numpy as np
from jax import lax
from jax.experimental import pallas as pl
from jax.experimental.pallas import tpu as pltpu

F32 = jnp.float32
BF16 = jnp.bfloat16

LN_EPS = 1e-5
RMS_EPS = 1e-5
REL_BUCKETS = 32
REL_MAX_DIST = 128
MASKED = -1e30
LOG2E = math.log2(math.e)
VMEM_LIMIT_BYTES = 56 * 1024 * 1024
LANES = 128
BF16_ROWS = 16


def _cparams(sem):
    return pltpu.CompilerParams(dimension_semantics=sem, vmem_limit_bytes=VMEM_LIMIT_BYTES)


def _sigmoid(x):
    return 1.0 / (1.0 + jnp.exp(-x))


def _dot(a, b):
    return jnp.dot(a, b, preferred_element_type=F32)


def _dot_nt(a, b):
    return lax.dot_general(a, b, (((1,), (1,)), ((), ())), preferred_element_type=F32)


def _dot_tn(a, b):
    return lax.dot_general(a, b, (((0,), (0,)), ((), ())), preferred_element_type=F32)


def _layer_norm(x, g, b):
    mu = jnp.mean(x, axis=-1, keepdims=True)
    xc = x - mu
    var = jnp.mean(xc * xc, axis=-1, keepdims=True)
    return xc * lax.rsqrt(var + LN_EPS) * g + b


def _ada_kernel(c_ref, w_ref, b_ref, o_ref):
    c = c_ref[...]
    s = c * _sigmoid(c)
    o_ref[...] = jnp.dot(s, w_ref[...], preferred_element_type=F32,
                         precision=lax.Precision.HIGHEST) + b_ref[...]


def _ada_call(c, w, b):
    rows, d = c.shape
    width = w.shape[1]
    tn = 1152 if width % 1152 == 0 else width
    return pl.pallas_call(
        _ada_kernel,
        grid=(width // tn,),
        in_specs=[pl.BlockSpec((rows, d), lambda j: (0, 0)),
                  pl.BlockSpec((d, tn), lambda j: (0, j)),
                  pl.BlockSpec((1, tn), lambda j: (0, j))],
        out_specs=pl.BlockSpec((rows, tn), lambda j: (0, j)),
        out_shape=jax.ShapeDtypeStruct((rows, width), F32),
        compiler_params=_cparams(("arbitrary",)),
        name="ada_proj",
    )(c, w, b.reshape(1, width))


def _ada_spec(ada4, k, tm, tiles_per_seq):
    d = ada4.shape[-1]
    if ada4.shape[2] == 1:
        return pl.BlockSpec((None, None, 1, d), lambda i, *_: (k, i // tiles_per_seq, 0, 0))
    return pl.BlockSpec((None, None, tm, d), lambda i, *_: (k, 0, i, 0))


def _ffn_ln_kernel(x_ref, sh_ref, sc_ref, gt_ref, wg_ref, wu_ref, wo_ref, lg_ref, lb_ref, o_ref, *, alpha):
    x = x_ref[...]
    h = (x * (1.0 + sc_ref[...]) + sh_ref[...]).astype(BF16)
    gate = _dot(h, wg_ref[...])
    up = _dot(h, wu_ref[...])
    a = (gate * _sigmoid(gate) * up).astype(BF16)
    y = alpha * x + 0.5 * gt_ref[...] * _dot(a, wo_ref[...])
    o_ref[...] = _layer_norm(y, lg_ref[...], lb_ref[...])


def _resident(shape, index_map):
    return pl.BlockSpec(shape, index_map, pipeline_mode=pl.Buffered(1))


def _ffn_ln_call(x, ada4, sub, w_in, w_out, ln_g, ln_b, *, alpha, tm, tiles_per_seq):
    n, d = x.shape
    f = w_out.shape[0]
    return pl.pallas_call(
        functools.partial(_ffn_ln_kernel, alpha=alpha),
        grid=(n // tm,),
        in_specs=[pl.BlockSpec((tm, d), lambda i: (i, 0)),
                  _ada_spec(ada4, 3 * sub + 0, tm, tiles_per_seq),
                  _ada_spec(ada4, 3 * sub + 1, tm, tiles_per_seq),
                  _ada_spec(ada4, 3 * sub + 2, tm, tiles_per_seq),
                  _resident((d, f), lambda i: (0, 0)),
                  _resident((d, f), lambda i: (0, 1)),
                  _resident((f, d), lambda i: (0, 0)),
                  pl.BlockSpec((1, d), lambda i: (0, 0)),
                  pl.BlockSpec((1, d), lambda i: (0, 0))],
        out_specs=pl.BlockSpec((tm, d), lambda i: (i, 0)),
        out_shape=jax.ShapeDtypeStruct((n, d), F32),
        compiler_params=_cparams(("parallel",)),
        name="ffn_ln",
    )(x, ada4, ada4, ada4, w_in, w_in, w_out, ln_g.reshape(1, d), ln_b.reshape(1, d))


def _inproj_kernel(x_ref, sh_ref, sc_ref, w_ref, lbp_ref,
                   qa_ref, lf_ref, ka_ref, va_ref, ga_ref, q16_ref, kb_ref, k16_ref, vb_ref, v16_ref,
                   sga_ref, sgb_ref, *, layer, hg_scale, da_scale):
    h = (x_ref[...] * (1.0 + sc_ref[...]) + sh_ref[...]).astype(BF16)
    d = h.shape[1]

    def z(j):
        return _dot(h, w_ref[:, j * d:(j + 1) * d])

    z0 = z(0)
    qa_ref[...] = (z0 * _sigmoid(z0) * hg_scale).astype(BF16)
    p = lbp_ref[...]
    e = jnp.exp(p - jnp.max(p, axis=0, keepdims=True))
    lb = jnp.sum(e[:layer + 1], axis=0, keepdims=True) / jnp.sum(e, axis=0, keepdims=True)
    f = lb + (1.0 - lb) * _sigmoid(z(1))
    lf_ref[...] = jnp.log(f)
    ka_ref[...] = (1.0 - f).astype(BF16)
    va_ref[...] = z(2).astype(BF16)
    z3 = z(3)
    ga_ref[...] = (z3 * _sigmoid(z3)).astype(BF16)
    q16_ref[...] = (z(4) * da_scale).astype(BF16)
    z5 = z(5)
    kb_ref[...] = z5
    k16_ref[...] = z5.astype(BF16)
    z6 = z(6)
    vb_ref[...] = z6
    v16_ref[...] = z6.astype(BF16)
    sga_ref[...] = _sigmoid(z(7)).astype(BF16)
    sgb_ref[...] = _sigmoid(z(8)).astype(BF16)


def _inproj_call(x, ada4, w_in, lb_params, *, layer, hg_key, da_head_dim, tm, tiles_per_seq):
    n, d = x.shape
    assert w_in.shape[1] == 9 * d
    kern = functools.partial(_inproj_kernel, layer=layer, hg_scale=hg_key ** -0.5,
                             da_scale=da_head_dim ** -0.5 * LOG2E)
    out_dtypes = [BF16, F32, BF16, BF16, BF16, BF16, F32, BF16, F32, BF16, BF16, BF16]
    row_spec = pl.BlockSpec((tm, d), lambda i: (i, 0))
    return pl.pallas_call(
        kern,
        grid=(n // tm,),
        in_specs=[row_spec,
                  _ada_spec(ada4, 3, tm, tiles_per_seq),
                  _ada_spec(ada4, 4, tm, tiles_per_seq),
                  _resident(w_in.shape, lambda i: (0, 0)),
                  pl.BlockSpec(lb_params.shape, lambda i: (0, 0))],
        out_specs=[row_spec] * len(out_dtypes),
        out_shape=[jax.ShapeDtypeStruct((n, d), dt) for dt in out_dtypes],
        compiler_params=_cparams(("parallel",)),
        name="in_proj",
    )(x, ada4, ada4, w_in, lb_params)


def _hgrn_kernel(*refs, chunk, heads, has_s0):
    if has_s0:
        q_ref, g_ref, k_ref, v_ref, gt_ref, ng_ref, s0_ref, o_ref, s_ref = refs
    else:
        q_ref, g_ref, k_ref, v_ref, gt_ref, ng_ref, o_ref, s_ref = refs
        s0_ref = None
    C = chunk

    @pl.when(pl.program_id(1) == 0)
    def _():
        if has_s0:
            s_ref[...] = s0_ref[...]
        else:
            s_ref[...] = jnp.zeros(s_ref.shape, F32)

    row = lax.broadcasted_iota(jnp.int32, (C, LANES), 0)
    ti = lax.broadcasted_iota(jnp.int32, (C, C), 0)
    si = lax.broadcasted_iota(jnp.int32, (C, C), 1)
    shifts, halves = [], []
    hh = 1
    while hh < C:
        shifts.append((hh, row >= hh))
        right = (row & (2 * hh - 1)) >= hh
        pair = (((ti ^ si) & ~(2 * hh - 1)) == 0) & ((ti & (2 * hh - 1)) >= hh) & ((si & (2 * hh - 1)) < hh)
        halves.append((hh, right, pair))
        hh *= 2
    eye = lax.broadcasted_iota(jnp.int32, (LANES, LANES), 0) == lax.broadcasted_iota(jnp.int32, (LANES, LANES), 1)
    sign = jnp.uint32(0x80000000)
    ng = ng_ref[...]

    for h in range(heads):
        sl = slice(h * LANES, (h + 1) * LANES)
        b = g_ref[0, :, sl] * LOG2E
        for sh, keep in shifts:
            b = b + jnp.where(keep, pltpu.roll(b, sh, 0), 0.0)
        q = q_ref[0, :, sl].astype(F32)
        k = k_ref[0, :, sl].astype(F32)
        vh = v_ref[0, :, sl]
        a = jnp.zeros((C, C), F32)
        u = b
        for hh, right, pair in halves:
            d = b - jnp.where(right, pltpu.roll(u, hh, 0), u)
            e = jnp.exp2(lax.bitcast_convert_type(lax.bitcast_convert_type(d, jnp.uint32) | sign, F32))
            a = a + jnp.where(pair, _dot_nt((q * e).astype(BF16), (k * e).astype(BF16)), 0.0)
            u = jnp.where(right, u, pltpu.roll(u, C - hh, 0))
        b_last = u
        s_old = s_ref[0, h]
        diag = jnp.sum(q * k, axis=1, keepdims=True)
        o = (_dot((q * jnp.exp2(b)).astype(BF16), s_old.astype(BF16)) + _dot(a.astype(BF16), vh)
             + diag * vh.astype(F32))
        ms = jnp.mean(o * o, axis=1, keepdims=True)
        o_ref[0, :, sl] = (o * lax.rsqrt(ms + RMS_EPS) * ng * gt_ref[0, :, sl].astype(F32)).astype(BF16)
        bl_col = jnp.sum(jnp.where(eye, jnp.broadcast_to(b_last[0:1], (LANES, LANES)), 0.0),
                         axis=1, keepdims=True)
        k_st = (k * jnp.exp2(b_last - b)).astype(BF16)
        s_ref[0, h] = jnp.exp2(bl_col) * s_old + _dot_tn(k_st, vh)


def _hgrn_call(qa, lf, ka, va, ga, norm_g, s0, layer, *, chunk):
    bsz, t, w = qa.shape
    heads = w // LANES
    has_s0 = s0 is not None
    kern = functools.partial(_hgrn_kernel, chunk=chunk, heads=heads, has_s0=has_s0)
    tok = pl.BlockSpec((1, chunk, w), lambda b, c: (b, c, 0))
    st = pl.BlockSpec((1, heads, LANES, LANES), lambda b, c: (b, 0, 0, 0))
    in_specs = [tok, tok, tok, tok, tok, pl.BlockSpec((1, LANES), lambda b, c: (0, 0))]
    args = [qa, lf, ka, va, ga, norm_g.reshape(1, LANES)]
    if has_s0:
        in_specs.append(pl.BlockSpec((None, 1, heads, LANES, LANES), lambda b, c: (layer, b, 0, 0, 0)))
        args.append(s0)
    return pl.pallas_call(
        kern,
        grid=(bsz, t // chunk),
        in_specs=in_specs,
        out_specs=[tok, st],
        out_shape=[jax.ShapeDtypeStruct((bsz, t, w), BF16),
                   jax.ShapeDtypeStruct((bsz, heads, LANES, LANES), F32)],
        compiler_params=_cparams(("parallel", "arbitrary")),
        name="hgrn2",
    )(*args)


def _relbias_kernel(tab_ref, o_ref, *, base, stride):
    hm = pl.program_id(0)
    rows, cols = o_ref.shape
    off = base + pl.program_id(1) * stride
    n = off + lax.broadcasted_iota(jnp.int32, (rows, cols), 0) - lax.broadcasted_iota(jnp.int32, (rows, cols), 1)
    nn = jnp.maximum(n, 0)
    max_exact = REL_BUCKETS // 2
    nf = jnp.maximum(nn, 1).astype(F32)
    large = max_exact + (jnp.log(nf / max_exact) / math.log(REL_MAX_DIST / max_exact)
                         * (REL_BUCKETS - max_exact)).astype(jnp.int32)
    bucket = jnp.where(nn < max_exact, nn, jnp.minimum(large, REL_BUCKETS - 1))
    val = jnp.zeros((rows, cols), F32)
    for bk in range(REL_BUCKETS):
        val = jnp.where(bucket == bk, tab_ref[hm * REL_BUCKETS + bk], val)
    o_ref[...] = jnp.where(n < 0, MASKED, val * LOG2E)


def _relbias_call(tab_flat, heads, kinds, rows, cols, base, stride):
    return pl.pallas_call(
        functools.partial(_relbias_kernel, base=base, stride=stride),
        grid=(heads * 2, kinds),
        in_specs=[pl.BlockSpec(memory_space=pltpu.SMEM)],
        out_specs=pl.BlockSpec((None, None, None, rows, cols), lambda i, kd: (i // 2, kd, i % 2, 0, 0)),
        out_shape=jax.ShapeDtypeStruct((heads, kinds, 2, rows, cols), F32),
        compiler_params=_cparams(("arbitrary", "arbitrary")),
        name="rel_bias",
    )(tab_flat)


def _lambda(lamv_ref, lam_init):
    lv = lamv_ref[...]
    s1 = jnp.sum(lv[0:1] * lv[1:2], axis=1, keepdims=True)
    s2 = jnp.sum(lv[2:3] * lv[3:4], axis=1, keepdims=True)
    return jnp.exp(s1) - jnp.exp(s2) + lam_init


def _attn_prompt_kernel(lamv_ref, q_ref, k_ref, v_ref, bias_ref, g_ref, o_ref,
                        q2_scr, vx_scr, s_scr, mx_scr, acc_scr, *, t, half, lam_init):
    i = pl.program_id(2)

    @pl.when(i == 0)
    def _():
        vx_scr[:, 0:LANES] = v_ref[0]
        vx_scr[:, LANES:2 * LANES] = jnp.ones((vx_scr.shape[0], LANES), BF16)

    q = q_ref[0]
    lane = lax.broadcasted_iota(jnp.int32, q.shape, 1)
    zero = jnp.zeros(q.shape, q.dtype)
    q2_scr[0:t] = jnp.where(lane < half, q, zero)
    q2_scr[t:2 * t] = jnp.where(lane >= half, q, zero)
    mx_scr[...] = jnp.full(mx_scr.shape, MASKED, F32)
    n_sub = t // LANES

    def logits(j, carry):
        start = pl.multiple_of(j * t, t)
        s = _dot_nt(q2_scr[...], k_ref[0, pl.ds(start, t), :]) + bias_ref[0, jnp.minimum(i - j, 2)]
        s_scr[j] = s
        mx = mx_scr[...]
        for u in range(n_sub):
            mx = jnp.maximum(mx, s[:, u * LANES:(u + 1) * LANES])
        mx_scr[...] = mx
        return carry

    lax.fori_loop(0, i + 1, logits, 0)

    m = jnp.max(mx_scr[...], axis=1, keepdims=True)
    mx_scr[...] = jnp.broadcast_to(m, mx_scr.shape)
    acc_scr[...] = jnp.zeros(acc_scr.shape, F32)

    def weighted(j, carry):
        start = pl.multiple_of(j * t, t)
        s = s_scr[j]
        mb = mx_scr[...]
        p = jnp.concatenate([jnp.exp2(s[:, u * LANES:(u + 1) * LANES] - mb).astype(BF16)
                             for u in range(n_sub)], axis=1)
        acc_scr[...] += _dot(p, vx_scr[pl.ds(start, t), :])
        return carry

    lax.fori_loop(0, i + 1, weighted, 0)

    acc = acc_scr[...]
    on = acc[:, 0:LANES] / acc[:, LANES:2 * LANES]
    o = on[0:t] - _lambda(lamv_ref, lam_init) * on[t:2 * t]
    ms = jnp.mean(o * o, axis=1, keepdims=True)
    o_ref[0] = (o * lax.rsqrt(ms + RMS_EPS) * g_ref[...] * (1.0 - lam_init)).astype(BF16)


def _attn_prompt_call(q16, k16, v16, lamv, bias, subln_g, *, heads, t, lam_init):
    bsz, seq, w = q16.shape
    e = w // heads
    nq = seq // t
    kern = functools.partial(_attn_prompt_kernel, t=t, half=e // 2, lam_init=lam_init)
    return pl.pallas_call(
        kern,
        grid=(heads, bsz, nq),
        in_specs=[pl.BlockSpec(lamv.shape, lambda h, b, i: (0, 0)),
                  pl.BlockSpec((1, t, e), lambda h, b, i: (b, i, h)),
                  pl.BlockSpec((1, seq, e), lambda h, b, i: (b, 0, h)),
                  pl.BlockSpec((1, seq, e), lambda h, b, i: (b, 0, h)),
                  pl.BlockSpec((1, 3, 2 * t, t), lambda h, b, i: (h, 0, 0, 0)),
                  pl.BlockSpec((1, e), lambda h, b, i: (0, 0))],
        out_specs=pl.BlockSpec((1, t, e), lambda h, b, i: (b, i, h)),
        out_shape=jax.ShapeDtypeStruct((bsz, seq, w), BF16),
        scratch_shapes=[pltpu.VMEM((2 * t, e), BF16), pltpu.VMEM((seq, 2 * e), BF16),
                        pltpu.VMEM((nq, 2 * t, t), F32), pltpu.VMEM((2 * t, LANES), F32),
                        pltpu.VMEM((2 * t, 2 * e), F32)],
        compiler_params=_cparams(("arbitrary", "arbitrary", "arbitrary")),
        name="attn_prompt",
    )(lamv, q16, k16, v16, bias, subln_g.reshape(1, e))


def _attn_sample_kernel(pt_ref, lamv_ref, qh_ref, *rest, group, n_steps, heads, t_new, lam_init):
    k_refs, v_refs = rest[:group], rest[group:2 * group]
    kn_ref, vn_ref, bfar_ref, blast_ref, bnew_ref, g_ref, o_ref, m_scr, l_scr, acc_scr = rest[2 * group:]
    p = pl.program_id(1)
    page = k_refs[0].shape[0] // heads
    rows = qh_ref.shape[2] // 2

    @pl.when(p == 0)
    def _():
        m_scr[...] = jnp.full(m_scr.shape, MASKED, F32)
        l_scr[...] = jnp.zeros(l_scr.shape, F32)
        acc_scr[...] = jnp.zeros(acc_scr.shape, F32)

    def update(c, s, vals):
        m_old = m_scr[c]
        m_new = jnp.maximum(m_old, jnp.max(s, axis=1, keepdims=True))
        alpha = jnp.exp2(m_old - m_new)
        pr = jnp.exp2(s - m_new)
        l_scr[c] = alpha * l_scr[c] + jnp.sum(pr, axis=1, keepdims=True)
        acc_scr[c] = alpha * acc_scr[c] + _dot(pr.astype(BF16), vals)
        m_scr[c] = m_new

    def pair_pages(refs, c):
        even, odd = pl.ds(2 * c, page, stride=heads), pl.ds(2 * c + 1, page, stride=heads)
        return jnp.concatenate([jnp.concatenate([r[even, :], r[odd, :]], axis=1) for r in refs],
                               axis=0).astype(BF16)

    last = p == n_steps - 1
    for c in range(heads // 2):
        s = _dot_nt(qh_ref[0, c], pair_pages(k_refs, c)) + jnp.where(last, blast_ref[c], bfar_ref[c])
        update(c, s, pair_pages(v_refs, c))

    @pl.when(last)
    def _():
        lam = _lambda(lamv_ref, lam_init)
        for c in range(heads // 2):
            sl2 = slice(2 * c * LANES, (2 * c + 2) * LANES)
            update(c, _dot_nt(qh_ref[0, c], kn_ref[0, :, sl2]) + bnew_ref[c], vn_ref[0, :, sl2])
            on2 = acc_scr[c] / l_scr[c]
            for side in range(2):
                on = on2[side * rows:(side + 1) * rows, side * LANES:(side + 1) * LANES]
                o = on - lam * pltpu.roll(on, rows - t_new, 0)
                ms = jnp.mean(o * o, axis=1, keepdims=True)
                sl = slice((2 * c + side) * LANES, (2 * c + side + 1) * LANES)
                o_ref[0, :, sl] = o * lax.rsqrt(ms + RMS_EPS) * g_ref[...] * (1.0 - lam_init)


def _attn_sample_call(page_table, lamv, qh, cache_k, cache_v, layer, k_new, v_new, b_far, b_last, b_new, subln_g,
                      *, group, t_new, lam_init):
    bsz, pairs, rows2, e2 = qh.shape
    heads, rows, e = 2 * pairs, rows2 // 2, e2 // 2
    n_pages = page_table.shape[1]
    n_steps = n_pages // group
    page_rows = cache_k.shape[2]
    w = heads * e
    rows_new = k_new.shape[1]
    kern = functools.partial(_attn_sample_kernel, group=group, n_steps=n_steps, heads=heads, t_new=t_new,
                             lam_init=lam_init)

    def page_spec(r):
        return pl.BlockSpec((None, None, page_rows, e),
                            lambda b, p, pt: (layer, pt[b * n_pages + p * group + r], 0, 0))

    fixed = lambda shape: pl.BlockSpec(shape, lambda b, p, pt: (0,) * len(shape))
    grid_spec = pltpu.PrefetchScalarGridSpec(
        num_scalar_prefetch=1,
        grid=(bsz, n_steps),
        in_specs=[fixed(lamv.shape),
                  pl.BlockSpec((1, pairs, rows2, e2), lambda b, p, pt: (b, 0, 0, 0))]
                 + [page_spec(r) for r in range(group)] * 2
                 + [pl.BlockSpec((1, rows_new, w), lambda b, p, pt: (b, 0, 0)),
                    pl.BlockSpec((1, rows_new, w), lambda b, p, pt: (b, 0, 0)),
                    fixed(b_far.shape), fixed(b_last.shape), fixed(b_new.shape), fixed((1, e))],
        out_specs=pl.BlockSpec((1, rows, w), lambda b, p, pt: (b, 0, 0)),
        scratch_shapes=[pltpu.VMEM((pairs, rows2, 1), F32), pltpu.VMEM((pairs, rows2, 1), F32),
                        pltpu.VMEM((pairs, rows2, e2), F32)],
    )
    return pl.pallas_call(
        kern,
        grid_spec=grid_spec,
        out_shape=jax.ShapeDtypeStruct((bsz, rows, w), F32),
        compiler_params=_cparams(("parallel", "arbitrary")),
        name="attn_sample",
    )(page_table.reshape(-1), lamv, qh, *([cache_k] * group), *([cache_v] * group), k_new, v_new,
      b_far, b_last, b_new, subln_g.reshape(1, e))


def _merge_kernel(x_ref, gt_ref, oa_ref, ob_ref, sga_ref, sgb_ref, wpa_ref, wpb_ref, wo_ref, lg_ref, lb_ref,
                  o_ref, *, alpha):
    m = (sga_ref[...].astype(F32) * _dot(oa_ref[...], wpa_ref[...])
         + sgb_ref[...].astype(F32) * _dot(ob_ref[...], wpb_ref[...]))
    r = _dot(m.astype(BF16), wo_ref[...])
    o_ref[...] = _layer_norm(alpha * x_ref[...] + gt_ref[...] * r, lg_ref[...], lb_ref[...])


def _merge_call(x, ada4, oa, ob, sga, sgb, w_pa, w_pb, w_o, ln_g, ln_b, *, alpha, tm, tiles_per_seq):
    n, d = x.shape
    row = pl.BlockSpec((tm, d), lambda i: (i, 0))
    wsp = pl.BlockSpec((d, d), lambda i: (0, 0))
    vec = pl.BlockSpec((1, d), lambda i: (0, 0))
    return pl.pallas_call(
        functools.partial(_merge_kernel, alpha=alpha),
        grid=(n // tm,),
        in_specs=[row, _ada_spec(ada4, 5, tm, tiles_per_seq), row, row, row, row, wsp, wsp, wsp, vec, vec],
        out_specs=row,
        out_shape=jax.ShapeDtypeStruct((n, d), F32),
        compiler_params=_cparams(("parallel",)),
        name="merge",
    )(x, ada4, oa, ob, sga, sgb, w_pa, w_pb, w_o, ln_g.reshape(1, d), ln_b.reshape(1, d))


def _row_tile(n, pref):
    tm = min(n, pref)
    while n % tm:
        tm //= 2
    return tm


def kernel(x_prompt, x_sample, cache_k, cache_v, state_hgrn, page_table, c_prompt, c_sample, w_ada, b_ada, ln_g, ln_b, ffn_w_in, ffn_w_out, w_in, hg_lower_bound, hg_norm_g, da_lambda_q1, da_lambda_k1, da_lambda_q2, da_lambda_k2, da_subln_g, rel_bias, w_pa, w_pb, w_o):
    depth = w_in.shape[0]
    bp, seq, d = x_prompt.shape
    bs, t_new, _ = x_sample.shape
    n_pool, page, da_heads, da_w = cache_k.shape[1:]
    hg_heads, hg_key = state_hgrn.shape[2], state_hgrn.shape[3]
    n_pages = page_table.shape[1]
    alpha = (2.0 * depth) ** 0.25
    t_attn = min(seq, 512)
    assert t_attn >= REL_MAX_DIST - 1 and page >= REL_MAX_DIST - 1
    assert da_w == LANES and hg_key == LANES and d == hg_heads * LANES == da_heads * LANES
    assert 2 * t_new <= BF16_ROWS

    group = 16
    while n_pages % group:
        group //= 2
    tab = jnp.transpose(rel_bias.astype(F32), (2, 1, 0)).reshape(-1)
    bias_p = _relbias_call(tab, da_heads, 3, t_attn, t_attn, 0, t_attn).reshape(da_heads, 3, 2 * t_attn, t_attn)
    far_tile = _relbias_call(tab, da_heads, 1, 8, LANES, 2 * REL_MAX_DIST + LANES, 0)
    last_tile = _relbias_call(tab, da_heads, 1, 8, LANES, page, 0)
    new_tile = _relbias_call(tab, da_heads, 1, 8, LANES, 0, 0)

    def sample_rows(tile, cols):
        a = tile[:, 0, :, :t_new, :cols].reshape(da_heads, 2 * t_new, cols)
        return jnp.pad(a, ((0, 0), (0, BF16_ROWS - 2 * t_new), (0, 0)))

    def pair_rows(a):
        return a.reshape(da_heads // 2, 2 * BF16_ROWS, a.shape[-1])

    sb_far = sample_rows(far_tile, 1)
    sb_last = jnp.concatenate([jnp.broadcast_to(sb_far, (da_heads, BF16_ROWS, (group - 1) * page)),
                               sample_rows(last_tile, page)], axis=2)
    sb_new = pair_rows(sample_rows(new_tile, BF16_ROWS))
    sb_far, sb_last = pair_rows(sb_far), pair_rows(sb_last)

    n_p, n_s = bp * seq, bs * t_new
    tm_p = _row_tile(seq, 512)
    tps_p = seq // tm_p
    tm_s = _row_tile(n_s, 512)
    hg_chunk = min(seq, 64)
    t_pad = BF16_ROWS
    map_of_lane = (jnp.arange(LANES) // (LANES // 2))[None, :]
    map_of_row = (jnp.arange(2 * t_new) // t_new)[:, None]
    own_map = (map_of_lane == map_of_row)[None, None]

    y_p = x_prompt.reshape(n_p, d)
    y_s = x_sample.reshape(n_s, d)
    cache_k2 = cache_k.reshape(depth, n_pool, page * da_heads, da_w)
    cache_v2 = cache_v.reshape(depth, n_pool, page * da_heads, da_w)
    outs = [[] for _ in range(6)]
    for l in range(depth):
        lam_init = 0.8 - 0.6 * math.exp(-0.3 * l)
        lamv = jnp.stack([da_lambda_q1[l], da_lambda_k1[l], da_lambda_q2[l], da_lambda_k2[l]]).astype(F32)
        ada = _ada_call(jnp.concatenate([c_prompt, c_sample]).astype(F32), w_ada[l], b_ada[l])
        ada_p = jnp.transpose(ada[:bp].reshape(bp, 9, d), (1, 0, 2)).reshape(9, bp, 1, d)
        ada_s = jnp.transpose(jnp.repeat(ada[bp:], t_new, axis=0).reshape(n_s, 9, d), (1, 0, 2)).reshape(9, 1, n_s, d)
        wf_in = ffn_w_in[l].astype(BF16)
        wf_out = ffn_w_out[l].astype(BF16)
        w_in_l = w_in[l].astype(BF16)
        wpa, wpb, wo = w_pa[l].astype(BF16), w_pb[l].astype(BF16), w_o[l].astype(BF16)

        def trunk(y, ada4, tm, tps, attend, hgrn):
            x1 = _ffn_ln_call(y, ada4, 0, wf_in[0], wf_out[0], ln_g[l, 0], ln_b[l, 0],
                              alpha=alpha, tm=tm, tiles_per_seq=tps)
            qa, lf, ka, va, ga, q16, kb, k16, vb, v16, sga, sgb = _inproj_call(
                x1, ada4, w_in_l, hg_lower_bound.astype(F32), layer=l, hg_key=hg_key,
                da_head_dim=da_w // 2, tm=min(tm, 256), tiles_per_seq=tps * (tm // min(tm, 256)))
            oa, s_new = hgrn(qa, lf, ka, va, ga)
            ob = attend(q16, k16, v16)
            x2 = _merge_call(x1, ada4, oa, ob, sga, sgb, wpa, wpb, wo, ln_g[l, 1], ln_b[l, 1],
                             alpha=alpha, tm=tm, tiles_per_seq=tps)
            y_new = _ffn_ln_call(x2, ada4, 2, wf_in[1], wf_out[1], ln_g[l, 2], ln_b[l, 2],
                                 alpha=alpha, tm=tm, tiles_per_seq=tps)
            return y_new, kb, vb, s_new

        def hgrn_p(qa, lf, ka, va, ga):
            r3 = lambda a: a.reshape(bp, seq, d)
            oa, s_new = _hgrn_call(r3(qa), r3(lf), r3(ka), r3(va), r3(ga), hg_norm_g[l], None, l, chunk=hg_chunk)
            return oa.reshape(n_p, d), s_new

        def attend_p(q16, k16, v16):
            r3 = lambda a: a.reshape(bp, seq, d)
            ob = _attn_prompt_call(r3(q16), r3(k16), r3(v16), lamv, bias_p, da_subln_g[l],
                                   heads=da_heads, t=t_attn, lam_init=lam_init)
            return ob.reshape(n_p, d)

        def hgrn_s(qa, lf, ka, va, ga):
            pad = lambda a: jnp.pad(a.reshape(bs, t_new, d), ((0, 0), (0, t_pad - t_new), (0, 0)))
            oa, s_new = _hgrn_call(pad(qa), pad(lf), pad(ka), pad(va), pad(ga), hg_norm_g[l],
                                   state_hgrn.astype(F32), l, chunk=t_pad)
            return oa[:, :t_new].reshape(n_s, d), s_new

        def attend_s(q16, k16, v16):
            qt = jnp.transpose(q16.reshape(bs, t_new, da_heads, da_w), (0, 2, 1, 3))
            qh = jnp.where(own_map, jnp.concatenate([qt, qt], axis=2), jnp.zeros((), BF16))
            qh = jnp.pad(qh, ((0, 0), (0, 0), (0, BF16_ROWS - 2 * t_new), (0, 0)))
            qh = qh.reshape(bs, da_heads // 2, 2, BF16_ROWS, da_w)
            zq = jnp.zeros_like(qh[:, :, 0])
            qh = jnp.concatenate([jnp.concatenate([qh[:, :, 0], zq], axis=-1),
                                  jnp.concatenate([zq, qh[:, :, 1]], axis=-1)], axis=2)
            padn = lambda a: jnp.pad(a.reshape(bs, t_new, d), ((0, 0), (0, BF16_ROWS - t_new), (0, 0)))
            ob = _attn_sample_call(page_table.astype(jnp.int32), lamv, qh, cache_k2, cache_v2, l,
                                   padn(k16), padn(v16), sb_far, sb_last, sb_new, da_subln_g[l],
                                   group=group, t_new=t_new, lam_init=lam_init)
            return ob[:, :t_new].reshape(n_s, d).astype(BF16)

        y_p, k_p, v_p, s_p = trunk(y_p, ada_p, tm_p, tps_p, attend_p, hgrn_p)
        y_s, k_s, v_s, s_s = trunk(y_s, ada_s, tm_s, 1, attend_s, hgrn_s)
        for lst, val in zip(outs, (k_p.reshape(bp, seq, da_heads, da_w), v_p.reshape(bp, seq, da_heads, da_w), s_p,
                                   k_s.reshape(bs, t_new, da_heads, da_w), v_s.reshape(bs, t_new, da_heads, da_w), s_s)):
            lst.append(val)

    k_prompt, v_prompt, st_prompt, k_sample, v_sample, st_sample = (
        o[0][None] if len(o) == 1 else jnp.stack(o) for o in outs)
    return (y_p.reshape(bp, seq, d), y_s.reshape(bs, t_new, d), k_prompt, v_prompt, st_prompt,
            k_sample, v_sample, st_sample)
```

```python
import functools
import math

import jax
import jax.numpy as jnp
import numpy as np
from jax import lax
from jax.experimental import pallas as pl
from jax.experimental.pallas import tpu as pltpu

F32 = jnp.float32
BF16 = jnp.bfloat16

LN_EPS = 1e-5
RMS_EPS = 1e-5
REL_BUCKETS = 32
REL_MAX_DIST = 128
MASKED = -1e30
LOG2E = math.log2(math.e)
VMEM_LIMIT_BYTES = 56 * 1024 * 1024
LANES = 128
BF16_ROWS = 16


def _cparams(sem):
    return pltpu.CompilerParams(dimension_semantics=sem, vmem_limit_bytes=VMEM_LIMIT_BYTES)


def _sigmoid(x):
    return 1.0 / (1.0 + jnp.exp(-x))


def _dot(a, b):
    return jnp.dot(a, b, preferred_element_type=F32)


def _dot_nt(a, b):
    return lax.dot_general(a, b, (((1,), (1,)), ((), ())), preferred_element_type=F32)


def _dot_tn(a, b):
    return lax.dot_general(a, b, (((0,), (0,)), ((), ())), preferred_element_type=F32)


def _layer_norm(x, g, b):
    mu = jnp.mean(x, axis=-1, keepdims=True)
    xc = x - mu
    var = jnp.mean(xc * xc, axis=-1, keepdims=True)
    return xc * lax.rsqrt(var + LN_EPS) * g + b


def _ada_kernel(c_ref, w_ref, b_ref, o_ref):
    c = c_ref[...]
    s = c * _sigmoid(c)
    o_ref[...] = jnp.dot(s, w_ref[...], preferred_element_type=F32,
                         precision=lax.Precision.HIGHEST) + b_ref[...]


def _ada_call(c, w, b):
    rows, d = c.shape
    width = w.shape[1]
    tn = 1152 if width % 1152 == 0 else width
    return pl.pallas_call(
        _ada_kernel,
        grid=(width // tn,),
        in_specs=[pl.BlockSpec((rows, d), lambda j: (0, 0)),
                  pl.BlockSpec((d, tn), lambda j: (0, j)),
                  pl.BlockSpec((1, tn), lambda j: (0, j))],
        out_specs=pl.BlockSpec((rows, tn), lambda j: (0, j)),
        out_shape=jax.ShapeDtypeStruct((rows, width), F32),
        compiler_params=_cparams(("arbitrary",)),
        name="ada_proj",
    )(c, w, b.reshape(1, width))


def _ada_spec(ada4, k, tm, tiles_per_seq):
    d = ada4.shape[-1]
    if ada4.shape[2] == 1:
        return pl.BlockSpec((None, None, 1, d), lambda i, *_: (k, i // tiles_per_seq, 0, 0))
    return pl.BlockSpec((None, None, tm, d), lambda i, *_: (k, 0, i, 0))


def _ffn_ln_kernel(x_ref, sh_ref, sc_ref, gt_ref, wg_ref, wu_ref, wo_ref, lg_ref, lb_ref, o_ref, *, alpha):
    x = x_ref[...]
    h = (x * (1.0 + sc_ref[...]) + sh_ref[...]).astype(BF16)
    gate = _dot(h, wg_ref[...])
    up = _dot(h, wu_ref[...])
    a = (gate * _sigmoid(gate) * up).astype(BF16)
    y = alpha * x + 0.5 * gt_ref[...] * _dot(a, wo_ref[...])
    o_ref[...] = _layer_norm(y, lg_ref[...], lb_ref[...])


def _resident(shape, index_map):
    return pl.BlockSpec(shape, index_map, pipeline_mode=pl.Buffered(1))


def _ffn_ln_call(x, ada4, sub, w_in, w_out, ln_g, ln_b, *, alpha, tm, tiles_per_seq):
    n, d = x.shape
    f = w_out.shape[0]
    return pl.pallas_call(
        functools.partial(_ffn_ln_kernel, alpha=alpha),
        grid=(n // tm,),
        in_specs=[pl.BlockSpec((tm, d), lambda i: (i, 0)),
                  _ada_spec(ada4, 3 * sub + 0, tm, tiles_per_seq),
                  _ada_spec(ada4, 3 * sub + 1, tm, tiles_per_seq),
                  _ada_spec(ada4, 3 * sub + 2, tm, tiles_per_seq),
                  _resident((d, f), lambda i: (0, 0)),
                  _resident((d, f), lambda i: (0, 1)),
                  _resident((f, d), lambda i: (0, 0)),
                  pl.BlockSpec((1, d), lambda i: (0, 0)),
                  pl.BlockSpec((1, d), lambda i: (0, 0))],
        out_specs=pl.BlockSpec((tm, d), lambda i: (i, 0)),
        out_shape=jax.ShapeDtypeStruct((n, d), F32),
        compiler_params=_cparams(("parallel",)),
        name="ffn_ln",
    )(x, ada4, ada4, ada4, w_in, w_in, w_out, ln_g.reshape(1, d), ln_b.reshape(1, d))


def _inproj_kernel(x_ref, sh_ref, sc_ref, w_ref, lbp_ref,
                   qa_ref, lf_ref, ka_ref, va_ref, ga_ref, q16_ref, kb_ref, k16_ref, vb_ref, v16_ref,
                   sga_ref, sgb_ref, *, layer, hg_scale, da_scale):
    h = (x_ref[...] * (1.0 + sc_ref[...]) + sh_ref[...]).astype(BF16)
    d = h.shape[1]

    def z(j):
        return _dot(h, w_ref[:, j * d:(j + 1) * d])

    z0 = z(0)
    qa_ref[...] = (z0 * _sigmoid(z0) * hg_scale).astype(BF16)
    p = lbp_ref[...]
    e = jnp.exp(p - jnp.max(p, axis=0, keepdims=True))
    lb = jnp.sum(e[:layer + 1], axis=0, keepdims=True) / jnp.sum(e, axis=0, keepdims=True)
    f = lb + (1.0 - lb) * _sigmoid(z(1))
    lf_ref[...] = jnp.log(f)
    ka_ref[...] = (1.0 - f).astype(BF16)
    va_ref[...] = z(2).astype(BF16)
    z3 = z(3)
    ga_ref[...] = (z3 * _sigmoid(z3)).astype(BF16)
    q16_ref[...] = (z(4) * da_scale).astype(BF16)
    z5 = z(5)
    kb_ref[...] = z5
    k16_ref[...] = z5.astype(BF16)
    z6 = z(6)
    vb_ref[...] = z6
    v16_ref[...] = z6.astype(BF16)
    sga_ref[...] = _sigmoid(z(7)).astype(BF16)
    sgb_ref[...] = _sigmoid(z(8)).astype(BF16)


def _inproj_call(x, ada4, w_in, lb_params, *, layer, hg_key, da_head_dim, tm, tiles_per_seq):
    n, d = x.shape
    assert w_in.shape[1] == 9 * d
    kern = functools.partial(_inproj_kernel, layer=layer, hg_scale=hg_key ** -0.5,
                             da_scale=da_head_dim ** -0.5 * LOG2E)
    out_dtypes = [BF16, F32, BF16, BF16, BF16, BF16, F32, BF16, F32, BF16, BF16, BF16]
    row_spec = pl.BlockSpec((tm, d), lambda i: (i, 0))
    return pl.pallas_call(
        kern,
        grid=(n // tm,),
        in_specs=[row_spec,
                  _ada_spec(ada4, 3, tm, tiles_per_seq),
                  _ada_spec(ada4, 4, tm, tiles_per_seq),
                  _resident(w_in.shape, lambda i: (0, 0)),
                  pl.BlockSpec(lb_params.shape, lambda i: (0, 0))],
        out_specs=[row_spec] * len(out_dtypes),
        out_shape=[jax.ShapeDtypeStruct((n, d), dt) for dt in out_dtypes],
        compiler_params=_cparams(("parallel",)),
        name="in_proj",
    )(x, ada4, ada4, w_in, lb_params)


def _hgrn_kernel(*refs, chunk, heads, has_s0):
    if has_s0:
        q_ref, g_ref, k_ref, v_ref, gt_ref, ng_ref, s0_ref, o_ref, s_ref = refs
    else:
        q_ref, g_ref, k_ref, v_ref, gt_ref, ng_ref, o_ref, s_ref = refs
        s0_ref = None
    C = chunk

    @pl.when(pl.program_id(1) == 0)
    def _():
        if has_s0:
            s_ref[...] = s0_ref[...]
        else:
            s_ref[...] = jnp.zeros(s_ref.shape, F32)

    row = lax.broadcasted_iota(jnp.int32, (C, LANES), 0)
    ti = lax.broadcasted_iota(jnp.int32, (C, C), 0)
    si = lax.broadcasted_iota(jnp.int32, (C, C), 1)
    shifts, halves = [], []
    hh = 1
    while hh < C:
        shifts.append((hh, row >= hh))
        right = (row & (2 * hh - 1)) >= hh
        pair = (((ti ^ si) & ~(2 * hh - 1)) == 0) & ((ti & (2 * hh - 1)) >= hh) & ((si & (2 * hh - 1)) < hh)
        halves.append((hh, right, pair))
        hh *= 2
    eye = lax.broadcasted_iota(jnp.int32, (LANES, LANES), 0) == lax.broadcasted_iota(jnp.int32, (LANES, LANES), 1)
    sign = jnp.uint32(0x80000000)
    ng = ng_ref[...]

    for h in range(heads):
        sl = slice(h * LANES, (h + 1) * LANES)
        b = g_ref[0, :, sl] * LOG2E
        for sh, keep in shifts:
            b = b + jnp.where(keep, pltpu.roll(b, sh, 0), 0.0)
        q = q_ref[0, :, sl].astype(F32)
        k = k_ref[0, :, sl].astype(F32)
        vh = v_ref[0, :, sl]
        a = jnp.zeros((C, C), F32)
        u = b
        for hh, right, pair in halves:
            d = b - jnp.where(right, pltpu.roll(u, hh, 0), u)
            e = jnp.exp2(lax.bitcast_convert_type(lax.bitcast_convert_type(d, jnp.uint32) | sign, F32))
            a = a + jnp.where(pair, _dot_nt((q * e).astype(BF16), (k * e).astype(BF16)), 0.0)
            u = jnp.where(right, u, pltpu.roll(u, C - hh, 0))
        b_last = u
        s_old = s_ref[0, h]
        diag = jnp.sum(q * k, axis=1, keepdims=True)
        o = (_dot((q * jnp.exp2(b)).astype(BF16), s_old.astype(BF16)) + _dot(a.astype(BF16), vh)
             + diag * vh.astype(F32))
        ms = jnp.mean(o * o, axis=1, keepdims=True)
        o_ref[0, :, sl] = (o * lax.rsqrt(ms + RMS_EPS) * ng * gt_ref[0, :, sl].astype(F32)).astype(BF16)
        bl_col = jnp.sum(jnp.where(eye, jnp.broadcast_to(b_last[0:1], (LANES, LANES)), 0.0),
                         axis=1, keepdims=True)
        k_st = (k * jnp.exp2(b_last - b)).astype(BF16)
        s_ref[0, h] = jnp.exp2(bl_col) * s_old + _dot_tn(k_st, vh)


def _hgrn_call(qa, lf, ka, va, ga, norm_g, s0, layer, *, chunk):
    bsz, t, w = qa.shape
    heads = w // LANES
    has_s0 = s0 is not None
    kern = functools.partial(_hgrn_kernel, chunk=chunk, heads=heads, has_s0=has_s0)
    tok = pl.BlockSpec((1, chunk, w), lambda b, c: (b, c, 0))
    st = pl.BlockSpec((1, heads, LANES, LANES), lambda b, c: (b, 0, 0, 0))
    in_specs = [tok, tok, tok, tok, tok, pl.BlockSpec((1, LANES), lambda b, c: (0, 0))]
    args = [qa, lf, ka, va, ga, norm_g.reshape(1, LANES)]
    if has_s0:
        in_specs.append(pl.BlockSpec((None, 1, heads, LANES, LANES), lambda b, c: (layer, b, 0, 0, 0)))
        args.append(s0)
    return pl.pallas_call(
        kern,
        grid=(bsz, t // chunk),
        in_specs=in_specs,
        out_specs=[tok, st],
        out_shape=[jax.ShapeDtypeStruct((bsz, t, w), BF16),
                   jax.ShapeDtypeStruct((bsz, heads, LANES, LANES), F32)],
        compiler_params=_cparams(("parallel", "arbitrary")),
        name="hgrn2",
    )(*args)


def _relbias_kernel(tab_ref, o_ref, *, base, stride, kinds):
    hm = pl.program_id(0)
    kind = pl.program_id(1)
    rows, cols = o_ref.shape

    def tile(off):
        n = off + lax.broadcasted_iota(jnp.int32, (rows, cols), 0) - lax.broadcasted_iota(jnp.int32, (rows, cols), 1)
        nn = jnp.maximum(n, 0)
        max_exact = REL_BUCKETS // 2
        nf = jnp.maximum(nn, 1).astype(F32)
        large = max_exact + (jnp.log(nf / max_exact) / math.log(REL_MAX_DIST / max_exact)
                             * (REL_BUCKETS - max_exact)).astype(jnp.int32)
        bucket = jnp.where(nn < max_exact, nn, jnp.minimum(large, REL_BUCKETS - 1))
        val = jnp.zeros((rows, cols), F32)
        for bk in range(REL_BUCKETS):
            val = jnp.where(bucket == bk, tab_ref[hm * REL_BUCKETS + bk], val)
        o_ref[...] = jnp.where(n < 0, MASKED, val * LOG2E)

    def saturated():
        o_ref[...] = jnp.full((rows, cols), tab_ref[hm * REL_BUCKETS + REL_BUCKETS - 1] * LOG2E, F32)

    for kd in range(kinds):
        off = base + kd * stride
        body = saturated if off - (cols - 1) >= REL_MAX_DIST else functools.partial(tile, off)
        pl.when(kind == kd)(body)


def _relbias_call(tab_flat, heads, kinds, rows, cols, base, stride):
    return pl.pallas_call(
        functools.partial(_relbias_kernel, base=base, stride=stride, kinds=kinds),
        grid=(heads * 2, kinds),
        in_specs=[pl.BlockSpec(memory_space=pltpu.SMEM)],
        out_specs=pl.BlockSpec((None, None, None, rows, cols), lambda i, kd: (i // 2, kd, i % 2, 0, 0)),
        out_shape=jax.ShapeDtypeStruct((heads, kinds, 2, rows, cols), F32),
        compiler_params=_cparams(("arbitrary", "arbitrary")),
        name="rel_bias",
    )(tab_flat)


def _lambda(lamv_ref, lam_init):
    lv = lamv_ref[...]
    s1 = jnp.sum(lv[0:1] * lv[1:2], axis=1, keepdims=True)
    s2 = jnp.sum(lv[2:3] * lv[3:4], axis=1, keepdims=True)
    return jnp.exp(s1) - jnp.exp(s2) + lam_init


def _attn_prompt_kernel(lamv_ref, q_ref, k_ref, v_ref, bias_ref, g_ref, o_ref,
                        q2_scr, vx_scr, s_scr, mx_scr, acc_scr, *, t, half, lam_init):
    i = pl.program_id(2)

    @pl.when(i == 0)
    def _():
        vx_scr[:, 0:LANES] = v_ref[0]
        vx_scr[:, LANES:2 * LANES] = jnp.ones((vx_scr.shape[0], LANES), BF16)

    q = q_ref[0]
    lane = lax.broadcasted_iota(jnp.int32, q.shape, 1)
    zero = jnp.zeros(q.shape, q.dtype)
    q2_scr[0:t] = jnp.where(lane < half, q, zero)
    q2_scr[t:2 * t] = jnp.where(lane >= half, q, zero)
    mx_scr[...] = jnp.full(mx_scr.shape, MASKED, F32)
    n_sub = t // LANES

    def logits(j, carry):
        start = pl.multiple_of(j * t, t)
        s = _dot_nt(q2_scr[...], k_ref[0, pl.ds(start, t), :]) + bias_ref[0, jnp.minimum(i - j, 2)]
        s_scr[j] = s
        mx = mx_scr[...]
        for u in range(n_sub):
            mx = jnp.maximum(mx, s[:, u * LANES:(u + 1) * LANES])
        mx_scr[...] = mx
        return carry

    lax.fori_loop(0, i + 1, logits, 0)

    m = jnp.max(mx_scr[...], axis=1, keepdims=True)
    mx_scr[...] = jnp.broadcast_to(m, mx_scr.shape)
    acc_scr[...] = jnp.zeros(acc_scr.shape, F32)

    def weighted(j, carry):
        start = pl.multiple_of(j * t, t)
        s = s_scr[j]
        mb = mx_scr[...]
        p = jnp.concatenate([jnp.exp2(s[:, u * LANES:(u + 1) * LANES] - mb).astype(BF16)
                             for u in range(n_sub)], axis=1)
        acc_scr[...] += _dot(p, vx_scr[pl.ds(start, t), :])
        return carry

    lax.fori_loop(0, i + 1, weighted, 0)

    acc = acc_scr[...]
    on = acc[:, 0:LANES] / acc[:, LANES:2 * LANES]
    o = on[0:t] - _lambda(lamv_ref, lam_init) * on[t:2 * t]
    ms = jnp.mean(o * o, axis=1, keepdims=True)
    o_ref[0] = (o * lax.rsqrt(ms + RMS_EPS) * g_ref[...] * (1.0 - lam_init)).astype(BF16)


def _attn_prompt_call(q16, k16, v16, lamv, bias, subln_g, *, heads, t, lam_init):
    bsz, seq, w = q16.shape
    e = w // heads
    nq = seq // t
    kern = functools.partial(_attn_prompt_kernel, t=t, half=e // 2, lam_init=lam_init)
    return pl.pallas_call(
        kern,
        grid=(heads, bsz, nq),
        in_specs=[pl.BlockSpec(lamv.shape, lambda h, b, i: (0, 0)),
                  pl.BlockSpec((1, t, e), lambda h, b, i: (b, i, h)),
                  pl.BlockSpec((1, seq, e), lambda h, b, i: (b, 0, h)),
                  pl.BlockSpec((1, seq, e), lambda h, b, i: (b, 0, h)),
                  pl.BlockSpec((1, 3, 2 * t, t), lambda h, b, i: (h, 0, 0, 0)),
                  pl.BlockSpec((1, e), lambda h, b, i: (0, 0))],
        out_specs=pl.BlockSpec((1, t, e), lambda h, b, i: (b, i, h)),
        out_shape=jax.ShapeDtypeStruct((bsz, seq, w), BF16),
        scratch_shapes=[pltpu.VMEM((2 * t, e), BF16), pltpu.VMEM((seq, 2 * e), BF16),
                        pltpu.VMEM((nq, 2 * t, t), F32), pltpu.VMEM((2 * t, LANES), F32),
                        pltpu.VMEM((2 * t, 2 * e), F32)],
        compiler_params=_cparams(("arbitrary", "arbitrary", "arbitrary")),
        name="attn_prompt",
    )(lamv, q16, k16, v16, bias, subln_g.reshape(1, e))


def _attn_sample_kernel(pt_ref, lamv_ref, qh_ref, *rest, group, n_steps, heads, t_new, lam_init):
    k_refs, v_refs = rest[:group], rest[group:2 * group]
    kn_ref, vn_ref, bfar_ref, blast_ref, bnew_ref, g_ref, o_ref, m_scr, l_scr, acc_scr = rest[2 * group:]
    p = pl.program_id(1)
    page = k_refs[0].shape[0] // heads
    rows = qh_ref.shape[2] // 2

    @pl.when(p == 0)
    def _():
        m_scr[...] = jnp.full(m_scr.shape, MASKED, F32)
        l_scr[...] = jnp.zeros(l_scr.shape, F32)
        acc_scr[...] = jnp.zeros(acc_scr.shape, F32)

    def update(c, s, vals):
        m_old = m_scr[c]
        m_new = jnp.maximum(m_old, jnp.max(s, axis=1, keepdims=True))
        alpha = jnp.exp2(m_old - m_new)
        pr = jnp.exp2(s - m_new)
        l_scr[c] = alpha * l_scr[c] + jnp.sum(pr, axis=1, keepdims=True)
        acc_scr[c] = alpha * acc_scr[c] + _dot(pr.astype(BF16), vals)
        m_scr[c] = m_new

    def pair_pages(refs, c):
        even, odd = pl.ds(2 * c, page, stride=heads), pl.ds(2 * c + 1, page, stride=heads)
        return jnp.concatenate([jnp.concatenate([r[even, :], r[odd, :]], axis=1) for r in refs],
                               axis=0).astype(BF16)

    last = p == n_steps - 1
    for c in range(heads // 2):
        s = _dot_nt(qh_ref[0, c], pair_pages(k_refs, c)) + jnp.where(last, blast_ref[c], bfar_ref[c])
        update(c, s, pair_pages(v_refs, c))

    @pl.when(last)
    def _():
        lam = _lambda(lamv_ref, lam_init)
        for c in range(heads // 2):
            sl2 = slice(2 * c * LANES, (2 * c + 2) * LANES)
            update(c, _dot_nt(qh_ref[0, c], kn_ref[0, :, sl2]) + bnew_ref[c], vn_ref[0, :, sl2])
            on2 = acc_scr[c] / l_scr[c]
            for side in range(2):
                on = on2[side * rows:(side + 1) * rows, side * LANES:(side + 1) * LANES]
                o = on - lam * pltpu.roll(on, rows - t_new, 0)
                ms = jnp.mean(o * o, axis=1, keepdims=True)
                sl = slice((2 * c + side) * LANES, (2 * c + side + 1) * LANES)
                o_ref[0, :, sl] = o * lax.rsqrt(ms + RMS_EPS) * g_ref[...] * (1.0 - lam_init)


def _attn_sample_call(page_table, lamv, qh, cache_k, cache_v, layer, k_new, v_new, b_far, b_last, b_new, subln_g,
                      *, group, t_new, lam_init):
    bsz, pairs, rows2, e2 = qh.shape
    heads, rows, e = 2 * pairs, rows2 // 2, e2 // 2
    n_pages = page_table.shape[1]
    n_steps = n_pages // group
    page_rows = cache_k.shape[2]
    w = heads * e
    rows_new = k_new.shape[1]
    kern = functools.partial(_attn_sample_kernel, group=group, n_steps=n_steps, heads=heads, t_new=t_new,
                             lam_init=lam_init)

    def page_spec(r):
        return pl.BlockSpec((None, None, page_rows, e),
                            lambda b, p, pt: (layer, pt[b * n_pages + p * group + r], 0, 0))

    fixed = lambda shape: pl.BlockSpec(shape, lambda b, p, pt: (0,) * len(shape))
    grid_spec = pltpu.PrefetchScalarGridSpec(
        num_scalar_prefetch=1,
        grid=(bsz, n_steps),
        in_specs=[fixed(lamv.shape),
                  pl.BlockSpec((1, pairs, rows2, e2), lambda b, p, pt: (b, 0, 0, 0))]
                 + [page_spec(r) for r in range(group)] * 2
                 + [pl.BlockSpec((1, rows_new, w), lambda b, p, pt: (b, 0, 0)),
                    pl.BlockSpec((1, rows_new, w), lambda b, p, pt: (b, 0, 0)),
                    fixed(b_far.shape), fixed(b_last.shape), fixed(b_new.shape), fixed((1, e))],
        out_specs=pl.BlockSpec((1, rows, w), lambda b, p, pt: (b, 0, 0)),
        scratch_shapes=[pltpu.VMEM((pairs, rows2, 1), F32), pltpu.VMEM((pairs, rows2, 1), F32),
                        pltpu.VMEM((pairs, rows2, e2), F32)],
    )
    return pl.pallas_call(
        kern,
        grid_spec=grid_spec,
        out_shape=jax.ShapeDtypeStruct((bsz, rows, w), F32),
        compiler_params=_cparams(("parallel", "arbitrary")),
        name="attn_sample",
    )(page_table.reshape(-1), lamv, qh, *([cache_k] * group), *([cache_v] * group), k_new, v_new,
      b_far, b_last, b_new, subln_g.reshape(1, e))


def _merge_kernel(x_ref, gt_ref, oa_ref, ob_ref, sga_ref, sgb_ref, wpa_ref, wpb_ref, wo_ref, lg_ref, lb_ref,
                  o_ref, *, alpha):
    m = (sga_ref[...].astype(F32) * _dot(oa_ref[...], wpa_ref[...])
         + sgb_ref[...].astype(F32) * _dot(ob_ref[...], wpb_ref[...]))
    r = _dot(m.astype(BF16), wo_ref[...])
    o_ref[...] = _layer_norm(alpha * x_ref[...] + gt_ref[...] * r, lg_ref[...], lb_ref[...])


def _merge_call(x, ada4, oa, ob, sga, sgb, w_pa, w_pb, w_o, ln_g, ln_b, *, alpha, tm, tiles_per_seq):
    n, d = x.shape
    row = pl.BlockSpec((tm, d), lambda i: (i, 0))
    wsp = pl.BlockSpec((d, d), lambda i: (0, 0))
    vec = pl.BlockSpec((1, d), lambda i: (0, 0))
    return pl.pallas_call(
        functools.partial(_merge_kernel, alpha=alpha),
        grid=(n // tm,),
        in_specs=[row, _ada_spec(ada4, 5, tm, tiles_per_seq), row, row, row, row, wsp, wsp, wsp, vec, vec],
        out_specs=row,
        out_shape=jax.ShapeDtypeStruct((n, d), F32),
        compiler_params=_cparams(("parallel",)),
        name="merge",
    )(x, ada4, oa, ob, sga, sgb, w_pa, w_pb, w_o, ln_g.reshape(1, d), ln_b.reshape(1, d))


def _row_tile(n, pref):
    tm = min(n, pref)
    while n % tm:
        tm //= 2
    return tm


def kernel(x_prompt, x_sample, cache_k, cache_v, state_hgrn, page_table, c_prompt, c_sample, w_ada, b_ada, ln_g, ln_b, ffn_w_in, ffn_w_out, w_in, hg_lower_bound, hg_norm_g, da_lambda_q1, da_lambda_k1, da_lambda_q2, da_lambda_k2, da_subln_g, rel_bias, w_pa, w_pb, w_o):
    depth = w_in.shape[0]
    bp, seq, d = x_prompt.shape
    bs, t_new, _ = x_sample.shape
    n_pool, page, da_heads, da_w = cache_k.shape[1:]
    hg_heads, hg_key = state_hgrn.shape[2], state_hgrn.shape[3]
    n_pages = page_table.shape[1]
    alpha = (2.0 * depth) ** 0.25
    t_attn = min(seq, 512)
    assert t_attn >= REL_MAX_DIST - 1 and page >= REL_MAX_DIST - 1
    assert da_w == LANES and hg_key == LANES and d == hg_heads * LANES == da_heads * LANES
    assert 2 * t_new <= BF16_ROWS

    group = 16
    while n_pages % group:
        group //= 2
    tab = jnp.transpose(rel_bias.astype(F32), (2, 1, 0)).reshape(-1)
    bias_p = _relbias_call(tab, da_heads, 3, t_attn, t_attn, 0, t_attn).reshape(da_heads, 3, 2 * t_attn, t_attn)
    far_tile = _relbias_call(tab, da_heads, 1, 8, LANES, 2 * REL_MAX_DIST + LANES, 0)
    last_tile = _relbias_call(tab, da_heads, 1, 8, LANES, page, 0)
    new_tile = _relbias_call(tab, da_heads, 1, 8, LANES, 0, 0)

    def sample_rows(tile, cols):
        a = tile[:, 0, :, :t_new, :cols].reshape(da_heads, 2 * t_new, cols)
        return jnp.pad(a, ((0, 0), (0, BF16_ROWS - 2 * t_new), (0, 0)))

    def pair_rows(a):
        return a.reshape(da_heads // 2, 2 * BF16_ROWS, a.shape[-1])

    sb_far = sample_rows(far_tile, 1)
    sb_last = jnp.concatenate([jnp.broadcast_to(sb_far, (da_heads, BF16_ROWS, (group - 1) * page)),
                               sample_rows(last_tile, page)], axis=2)
    sb_new = pair_rows(sample_rows(new_tile, BF16_ROWS))
    sb_far, sb_last = pair_rows(sb_far), pair_rows(sb_last)

    n_p, n_s = bp * seq, bs * t_new
    tm_p = _row_tile(seq, 512)
    tps_p = seq // tm_p
    tm_s = _row_tile(n_s, 512)
    hg_chunk = min(seq, 64)
    t_pad = BF16_ROWS
    map_of_lane = (jnp.arange(LANES) // (LANES // 2))[None, :]
    map_of_row = (jnp.arange(2 * t_new) // t_new)[:, None]
    own_map = (map_of_lane == map_of_row)[None, None]

    y_p = x_prompt.reshape(n_p, d)
    y_s = x_sample.reshape(n_s, d)
    cache_k2 = cache_k.reshape(depth, n_pool, page * da_heads, da_w)
    cache_v2 = cache_v.reshape(depth, n_pool, page * da_heads, da_w)
    outs = [[] for _ in range(6)]
    for l in range(depth):
        lam_init = 0.8 - 0.6 * math.exp(-0.3 * l)
        lamv = jnp.stack([da_lambda_q1[l], da_lambda_k1[l], da_lambda_q2[l], da_lambda_k2[l]]).astype(F32)
        ada = _ada_call(jnp.concatenate([c_prompt, c_sample]).astype(F32), w_ada[l], b_ada[l])
        ada_p = jnp.transpose(ada[:bp].reshape(bp, 9, d), (1, 0, 2)).reshape(9, bp, 1, d)
        ada_s = jnp.repeat(jnp.transpose(ada[bp:].reshape(bs, 9, d), (1, 0, 2)), t_new, axis=1).reshape(9, 1, n_s, d)
        wf_in = ffn_w_in[l].astype(BF16)
        wf_out = ffn_w_out[l].astype(BF16)
        w_in_l = w_in[l].astype(BF16)
        wpa, wpb, wo = w_pa[l].astype(BF16), w_pb[l].astype(BF16), w_o[l].astype(BF16)

        def trunk(y, ada4, tm, tps, attend, hgrn):
            x1 = _ffn_ln_call(y, ada4, 0, wf_in[0], wf_out[0], ln_g[l, 0], ln_b[l, 0],
                              alpha=alpha, tm=tm, tiles_per_seq=tps)
            qa, lf, ka, va, ga, q16, kb, k16, vb, v16, sga, sgb = _inproj_call(
                x1, ada4, w_in_l, hg_lower_bound.astype(F32), layer=l, hg_key=hg_key,
                da_head_dim=da_w // 2, tm=min(tm, 256), tiles_per_seq=tps * (tm // min(tm, 256)))
            oa, s_new = hgrn(qa, lf, ka, va, ga)
            ob = attend(q16, k16, v16)
            x2 = _merge_call(x1, ada4, oa, ob, sga, sgb, wpa, wpb, wo, ln_g[l, 1], ln_b[l, 1],
                             alpha=alpha, tm=tm, tiles_per_seq=tps)
            y_new = _ffn_ln_call(x2, ada4, 2, wf_in[1], wf_out[1], ln_g[l, 2], ln_b[l, 2],
                                 alpha=alpha, tm=tm, tiles_per_seq=tps)
            return y_new, kb, vb, s_new

        def hgrn_p(qa, lf, ka, va, ga):
            r3 = lambda a: a.reshape(bp, seq, d)
            oa, s_new = _hgrn_call(r3(qa), r3(lf), r3(ka), r3(va), r3(ga), hg_norm_g[l], None, l, chunk=hg_chunk)
            return oa.reshape(n_p, d), s_new

        def attend_p(q16, k16, v16):
            r3 = lambda a: a.reshape(bp, seq, d)
            ob = _attn_prompt_call(r3(q16), r3(k16), r3(v16), lamv, bias_p, da_subln_g[l],
                                   heads=da_heads, t=t_attn, lam_init=lam_init)
            return ob.reshape(n_p, d)

        def hgrn_s(qa, lf, ka, va, ga):
            pad = lambda a: jnp.pad(a.reshape(bs, t_new, d), ((0, 0), (0, t_pad - t_new), (0, 0)))
            oa, s_new = _hgrn_call(pad(qa), pad(lf), pad(ka), pad(va), pad(ga), hg_norm_g[l],
                                   state_hgrn.astype(F32), l, chunk=t_pad)
            return oa[:, :t_new].reshape(n_s, d), s_new

        def attend_s(q16, k16, v16):
            qt = jnp.transpose(q16.reshape(bs, t_new, da_heads, da_w), (0, 2, 1, 3))
            qh = jnp.where(own_map, jnp.concatenate([qt, qt], axis=2), jnp.zeros((), BF16))
            qh = jnp.pad(qh, ((0, 0), (0, 0), (0, BF16_ROWS - 2 * t_new), (0, 0)))
            qh = qh.reshape(bs, da_heads // 2, 2, BF16_ROWS, da_w)
            zq = jnp.zeros_like(qh[:, :, 0])
            qh = jnp.concatenate([jnp.concatenate([qh[:, :, 0], zq], axis=-1),
                                  jnp.concatenate([zq, qh[:, :, 1]], axis=-1)], axis=2)
            padn = lambda a: jnp.pad(a.reshape(bs, t_new, d), ((0, 0), (0, BF16_ROWS - t_new), (0, 0)))
            ob = _attn_sample_call(page_table.astype(jnp.int32), lamv, qh, cache_k2, cache_v2, l,
                                   padn(k16), padn(v16), sb_far, sb_last, sb_new, da_subln_g[l],
                                   group=group, t_new=t_new, lam_init=lam_init)
            return ob[:, :t_new].reshape(n_s, d).astype(BF16)

        y_p, k_p, v_p, s_p = trunk(y_p, ada_p, tm_p, tps_p, attend_p, hgrn_p)
        y_s, k_s, v_s, s_s = trunk(y_s, ada_s, tm_s, 1, attend_s, hgrn_s)
        for lst, val in zip(outs, (k_p.reshape(bp, seq, da_heads, da_w), v_p.reshape(bp, seq, da_heads, da_w), s_p,
                                   k_s.reshape(bs, t_new, da_heads, da_w), v_s.reshape(bs, t_new, da_heads, da_w), s_s)):
            lst.append(val)

    k_prompt, v_prompt, st_prompt, k_sample, v_sample, st_sample = (
        o[0][None] if len(o) == 1 else jnp.stack(o) for o in outs)
    return (y_p.reshape(bp, seq, d), y_s.reshape(bs, t_new, d), k_prompt, v_prompt, st_prompt,
            k_sample, v_sample, st_sample)
```
